```python
import math
import jax
import jax.numpy as jnp
from jax import lax
import numpy as np

D_MODEL = 1024
BATCH = 4
SEQ = 4096
DEPTH = 4
DEC_BATCH = 128
DEC_SEQ = 1
PAST_LEN = 2048
PAGE_SIZE = 128

N_HEADS = 16
HEAD_DIM = D_MODEL // N_HEADS
MOBA_BLOCK = 256
MOBA_TOPK = 3
Q_BLOCK = 128
D_FF_DENSE = ((8 * D_MODEL // 3 + 127) // 128) * 128
N_EXPERTS = 8
EXPERT_TOPK = 2
D_FF_EXPERT = 7 * D_MODEL // 2
LN_EPS = 1e-5
DEEPNORM_ALPHA = (2.0 * DEPTH) ** 0.25
DEEPNORM_BETA = (8.0 * DEPTH) ** -0.25

kernel_name = 'hybrid_moba_stickbreaking_decoder_step'


def alibi_slopes():
    return jnp.exp2(-8.0 * jnp.arange(1, N_HEADS + 1, dtype=jnp.float32) / N_HEADS)


def layer_norm(x, g, b):
    xf = x.astype(jnp.float32)
    mu = jnp.mean(xf, axis=-1, keepdims=True)
    var = jnp.mean(jnp.square(xf - mu), axis=-1, keepdims=True)
    return ((xf - mu) * lax.rsqrt(var + LN_EPS) * g + b).astype(x.dtype)


def project_qkv(x, w):
    b, t, _ = x.shape
    qkv = (x @ w).reshape(b, t, 3, N_HEADS, HEAD_DIM)
    return qkv[:, :, 0], qkv[:, :, 1], qkv[:, :, 2]


def moba_blocks(k, v):
    b, length = k.shape[:2]
    lp = -(-length // MOBA_BLOCK) * MOBA_BLOCK
    pad = ((0, 0), (0, lp - length), (0, 0), (0, 0))
    kb = jnp.pad(k, pad).reshape(b, lp // MOBA_BLOCK, MOBA_BLOCK, N_HEADS, HEAD_DIM)
    vb = jnp.pad(v, pad).reshape(b, lp // MOBA_BLOCK, MOBA_BLOCK, N_HEADS, HEAD_DIM)
    kmean = jnp.mean(kb.astype(jnp.float32), axis=2)
    return kb, vb, kmean


def moba_core(q, q_pos, b_idx, kb, vb, kmean):
    nb = kb.shape[1]
    topk = min(MOBA_TOPK, nb)
    own = q_pos // MOBA_BLOCK
    gate = jnp.einsum('bthd,bnhd->bhtn', q.astype(jnp.float32), kmean[b_idx])
    fully_past = jnp.arange(nb)[None, :] < own[:, None]
    gate = jnp.where(fully_past[None, None], gate, -jnp.inf)
    _, top_idx = lax.top_k(gate, topk)
    top_ok = top_idx < own[None, None, :, None]
    own_idx = jnp.broadcast_to(own[None, None, :, None], top_idx.shape[:3] + (1,)).astype(top_idx.dtype)
    sel = jnp.concatenate([top_idx, own_idx], axis=-1)
    bi = b_idx[:, None, None, None]
    hi = jnp.arange(N_HEADS)[None, :, None, None]
    k_sel = kb[bi, sel, :, hi]
    v_sel = vb[bi, sel, :, hi]
    k_pos = sel[..., None] * MOBA_BLOCK + jnp.arange(MOBA_BLOCK)
    blk_ok = jnp.concatenate([top_ok, jnp.ones(own_idx.shape, dtype=bool)], axis=-1)
    dist = q_pos[None, None, :, None, None] - k_pos
    ok = blk_ok[..., None] & (dist >= 0)
    s = jnp.einsum('bthd,bhtjsd->bhtjs', q, k_sel).astype(jnp.float32) / math.sqrt(HEAD_DIM)
    s = s - alibi_slopes()[None, :, None, None, None] * dist.astype(jnp.float32)
    s = jnp.where(ok, s, -jnp.inf)
    bq, h, tq, j, sb = s.shape
    p = jax.nn.softmax(s.reshape(bq, h, tq, j * sb), axis=-1).reshape(s.shape)
    return jnp.einsum('bhtjs,bhtjsd->bthd', p.astype(v_sel.dtype), v_sel)


def moba_prompt(q, k, v):
    b, t = q.shape[:2]
    nq = t // Q_BLOCK
    kb, vb, kmean = moba_blocks(k, v)
    b_ids = jnp.repeat(jnp.arange(b, dtype=jnp.int32), nq)
    c_ids = jnp.tile(jnp.arange(nq, dtype=jnp.int32), b)

    def body(bc):
        bb, c = bc
        q_c = lax.dynamic_slice(q, (bb, c * Q_BLOCK, 0, 0), (1, Q_BLOCK, N_HEADS, HEAD_DIM))
        pos = c * Q_BLOCK + jnp.arange(Q_BLOCK, dtype=jnp.int32)
        return moba_core(q_c, pos, bb[None], kb, vb, kmean)[0]

    out = lax.map(body, (b_ids, c_ids))
    return out.reshape(b, t, N_HEADS, HEAD_DIM)


def moba_sample(q, k_new, v_new, past_k, past_v):
    k = jnp.concatenate([past_k, k_new], axis=1)
    v = jnp.concatenate([past_v, v_new], axis=1)
    kb, vb, kmean = moba_blocks(k, v)
    pos = past_k.shape[1] + jnp.arange(q.shape[1], dtype=jnp.int32)
    return moba_core(q, pos, jnp.arange(q.shape[0], dtype=jnp.int32), kb, vb, kmean)


def stick_breaking_core(q, k, v, q_pos, k_pos):
    z = jnp.einsum('bthd,bshd->bhts', q, k).astype(jnp.float32) / math.sqrt(HEAD_DIM)
    causal = (k_pos[None, :] < q_pos[:, None])[None, None]
    log_keep = jnp.where(causal, jax.nn.log_sigmoid(-z), 0.0)
    suffix = lax.cumsum(log_keep, axis=3, reverse=True) - log_keep
    a = jnp.where(causal, jnp.exp(jax.nn.log_sigmoid(z) + suffix), 0.0)
    return jnp.einsum('bhts,bshd->bthd', a.astype(v.dtype), v)


def sb_prompt(q, k, v):
    b, t = q.shape[:2]
    nq = t // Q_BLOCK
    qb = q.reshape(b, nq, Q_BLOCK, N_HEADS, HEAD_DIM).transpose(1, 0, 2, 3, 4)
    k_pos = jnp.arange(t, dtype=jnp.int32)

    def body(args):
        q_c, c = args
        pos = c * Q_BLOCK + jnp.arange(Q_BLOCK, dtype=jnp.int32)
        return stick_breaking_core(q_c, k, v, pos, k_pos)

    out = lax.map(body, (qb, jnp.arange(nq, dtype=jnp.int32)))
    return out.transpose(1, 0, 2, 3, 4).reshape(b, t, N_HEADS, HEAD_DIM)


def sb_sample(q, k_new, v_new, past_k, past_v):
    k = jnp.concatenate([past_k, k_new], axis=1)
    v = jnp.concatenate([past_v, v_new], axis=1)
    past = past_k.shape[1]
    q_pos = past + jnp.arange(q.shape[1], dtype=jnp.int32)
    k_pos = jnp.arange(k.shape[1], dtype=jnp.int32)
    return stick_breaking_core(q, k, v, q_pos, k_pos)


def swiglu(x, w1, w3, w2):
    return (jax.nn.silu(x @ w1) * (x @ w3)) @ w2


def moe_swiglu(x, w_router, w1, w3, w2):
    shp = x.shape
    xt = x.reshape(-1, shp[-1])
    logits = (xt @ w_router).astype(jnp.float32)
    top_val, top_idx = lax.top_k(logits, EXPERT_TOPK)
    gates = jax.nn.softmax(top_val, axis=-1)
    combine = jnp.sum(jax.nn.one_hot(top_idx, N_EXPERTS, dtype=jnp.float32) * gates[..., None], axis=1).astype(x.dtype)
    y = jnp.zeros_like(xt)
    for e in range(N_EXPERTS):
        y = y + combine[:, e:e + 1] * swiglu(xt, w1[e], w3[e], w2[e])
    return y.reshape(shp)


def merge_heads(o):
    b, t = o.shape[:2]
    return o.reshape(b, t, N_HEADS * HEAD_DIM)


def setup_inputs(seed: int = 0) -> dict:
    key = jax.random.key(seed)
    ks = jax.random.split(key, 24)
    f32 = jnp.float32
    d = D_MODEL
    n_pages = PAST_LEN // PAGE_SIZE
    n_used = DEC_BATCH * n_pages
    n_pool = (5 * n_used + 3) // 4
    n_dense = (DEPTH + 1) // 2
    n_moe = DEPTH // 2
    x_prompt = jax.random.normal(ks[0], (BATCH, SEQ, d), f32)
    x_sample = jax.random.normal(ks[1], (DEC_BATCH, DEC_SEQ, d), f32)
    cache_k = jax.random.normal(ks[2], (DEPTH, n_pool, PAGE_SIZE, N_HEADS, HEAD_DIM), f32)
    cache_v = jax.random.normal(ks[3], (DEPTH, n_pool, PAGE_SIZE, N_HEADS, HEAD_DIM), f32)
    perm = jax.random.permutation(ks[4], n_pool)
    page_table = perm[:n_used].reshape(DEC_BATCH, n_pages).astype(jnp.int32)
    col_scale = jnp.concatenate([jnp.ones((2 * d,), f32), jnp.full((d,), DEEPNORM_BETA, f32)])
    w_qkv = jax.random.normal(ks[5], (DEPTH, d, 3 * d), f32) * d ** -0.5 * col_scale
    w_o = jax.random.normal(ks[6], (DEPTH, d, d), f32) * d ** -0.5 * DEEPNORM_BETA
    ln1_g = 1.0 + 0.02 * jax.random.normal(ks[7], (DEPTH, d), f32)
    ln1_b = 0.02 * jax.random.normal(ks[8], (DEPTH, d), f32)
    ln2_g = 1.0 + 0.02 * jax.random.normal(ks[9], (DEPTH, d), f32)
    ln2_b = 0.02 * jax.random.normal(ks[10], (DEPTH, d), f32)
    ffn_w1 = jax.random.normal(ks[11], (n_dense, d, D_FF_DENSE), f32) * d ** -0.5
    ffn_w3 = jax.random.normal(ks[12], (n_dense, d, D_FF_DENSE), f32) * d ** -0.5 * DEEPNORM_BETA
    ffn_w2 = jax.random.normal(ks[13], (n_dense, D_FF_DENSE, d), f32) * D_FF_DENSE ** -0.5 * DEEPNORM_BETA
    moe_router = jax.random.normal(ks[14], (n_moe, d, N_EXPERTS), f32) * d ** -0.5
    moe_w1 = jax.random.normal(ks[15], (n_moe, N_EXPERTS, d, D_FF_EXPERT), f32) * d ** -0.5
    moe_w3 = jax.random.normal(ks[16], (n_moe, N_EXPERTS, d, D_FF_EXPERT), f32) * d ** -0.5 * DEEPNORM_BETA
    moe_w2 = jax.random.normal(ks[17], (n_moe, N_EXPERTS, D_FF_EXPERT, d), f32) * D_FF_EXPERT ** -0.5 * DEEPNORM_BETA
    return {'x_prompt': x_prompt, 'x_sample': x_sample, 'cache_k': cache_k, 'cache_v': cache_v,
            'page_table': page_table, 'w_qkv': w_qkv, 'w_o': w_o, 'ln1_g': ln1_g, 'ln1_b': ln1_b,
            'ln2_g': ln2_g, 'ln2_b': ln2_b, 'ffn_w1': ffn_w1, 'ffn_w3': ffn_w3, 'ffn_w2': ffn_w2,
            'moe_router': moe_router, 'moe_w1': moe_w1, 'moe_w3': moe_w3, 'moe_w2': moe_w2}


def reference(x_prompt, x_sample, cache_k, cache_v, page_table, w_qkv, w_o, ln1_g, ln1_b,
              ln2_g, ln2_b, ffn_w1, ffn_w3, ffn_w2, moe_router, moe_w1, moe_w3, moe_w2):
    xp, xs = x_prompt, x_sample
    dec_b, n_pages = page_table.shape
    past_len = n_pages * cache_k.shape[2]
    new_kp, new_vp, new_ks, new_vs = [], [], [], []
    for i in range(DEPTH):
        qp, kp, vp = project_qkv(xp, w_qkv[i])
        qs, ks_, vs = project_qkv(xs, w_qkv[i])
        past_k = cache_k[i, page_table].reshape(dec_b, past_len, N_HEADS, HEAD_DIM)
        past_v = cache_v[i, page_table].reshape(dec_b, past_len, N_HEADS, HEAD_DIM)
        if i % 2 == 0:
            op = moba_prompt(qp, kp, vp)
            os_ = moba_sample(qs, ks_, vs, past_k, past_v)
        else:
            op = sb_prompt(qp, kp, vp)
            os_ = sb_sample(qs, ks_, vs, past_k, past_v)
        xp = layer_norm(DEEPNORM_ALPHA * xp + merge_heads(op) @ w_o[i], ln1_g[i], ln1_b[i])
        xs = layer_norm(DEEPNORM_ALPHA * xs + merge_heads(os_) @ w_o[i], ln1_g[i], ln1_b[i])
        j = i // 2
        if i % 2 == 0:
            fp = swiglu(xp, ffn_w1[j], ffn_w3[j], ffn_w2[j])
            fs = swiglu(xs, ffn_w1[j], ffn_w3[j], ffn_w2[j])
        else:
            fp = moe_swiglu(xp, moe_router[j], moe_w1[j], moe_w3[j], moe_w2[j])
            fs = moe_swiglu(xs, moe_router[j], moe_w1[j], moe_w3[j], moe_w2[j])
        xp = layer_norm(DEEPNORM_ALPHA * xp + fp, ln2_g[i], ln2_b[i])
        xs = layer_norm(DEEPNORM_ALPHA * xs + fs, ln2_g[i], ln2_b[i])
        new_kp.append(kp)
        new_vp.append(vp)
        new_ks.append(ks_)
        new_vs.append(vs)
    return (xp, xs, jnp.stack(new_kp), jnp.stack(new_vp), jnp.stack(new_ks), jnp.stack(new_vs))
```

```python
import functools

import jax
import jax.numpy as jnp
from jax import lax
from jax.experimental import pallas as pl
from jax.experimental.pallas import tpu as pltpu

N_HEADS = 16
MOBA_BLOCK = 256
MOBA_TOPK = 3
EXPERT_TOPK = 2
LN_EPS = 1e-5
NEG = -1e30
LANES = 128
VMEM_LIMIT = 56 * 1024 * 1024

F32 = jnp.float32
BF16 = jnp.bfloat16
_NT = (((1,), (1,)), ((), ()))
_HIGHEST = lax.Precision.HIGHEST


def _params(n_grid_dims):
    return pltpu.CompilerParams(
        dimension_semantics=("arbitrary",) * n_grid_dims,
        vmem_limit_bytes=VMEM_LIMIT)


def _resident(block_shape, index_map):
    return pl.BlockSpec(block_shape, index_map, pipeline_mode=pl.Buffered(1))


def _layer_norm(y, g, b):
    mu = jnp.mean(y, axis=-1, keepdims=True)
    yc = y - mu
    var = jnp.mean(yc * yc, axis=-1, keepdims=True)
    return yc * lax.rsqrt(var + LN_EPS) * g + b


def _silu(x):
    return x * jax.nn.sigmoid(x)


def _qkv_prompt_body(x_ref, wk_ref, wv_ref, wqt_ref, wvt_ref,
                     k_ref, v_ref, kb_ref, qt_ref, vt_ref, km_ref, *, q_scale):
    xb = x_ref[...].astype(BF16)
    k = jnp.dot(xb, wk_ref[...], preferred_element_type=F32)
    v = jnp.dot(xb, wv_ref[...], preferred_element_type=F32)
    k_ref[...] = k
    v_ref[...] = v
    kb_ref[...] = k.astype(BF16)
    km_ref[0] = jnp.mean(k, axis=0, keepdims=True)
    qt = lax.dot_general(wqt_ref[...], xb, _NT, preferred_element_type=F32)
    qt_ref[0] = qt * q_scale
    vt = lax.dot_general(wvt_ref[...], xb, _NT, preferred_element_type=F32)
    vt_ref[0] = vt.astype(BF16)


def _qkv_prompt(x, wk, wv, wqt, wvt, q_scale):
    m, d = x.shape
    blk = MOBA_BLOCK
    nt = m // blk
    w_spec = _resident((d, d), lambda i: (0, 0))
    return pl.pallas_call(
        functools.partial(_qkv_prompt_body, q_scale=q_scale),
        grid=(nt,),
        in_specs=[pl.BlockSpec((blk, d), lambda i: (i, 0)), w_spec, w_spec, w_spec, w_spec],
        out_specs=[
            pl.BlockSpec((blk, d), lambda i: (i, 0)),
            pl.BlockSpec((blk, d), lambda i: (i, 0)),
            pl.BlockSpec((blk, d), lambda i: (i, 0)),
            pl.BlockSpec((1, d, blk), lambda i: (i, 0, 0)),
            pl.BlockSpec((1, d, blk), lambda i: (i, 0, 0)),
            pl.BlockSpec((1, 1, d), lambda i: (i, 0, 0)),
        ],
        out_shape=[
            jax.ShapeDtypeStruct((m, d), F32),
            jax.ShapeDtypeStruct((m, d), F32),
            jax.ShapeDtypeStruct((m, d), BF16),
            jax.ShapeDtypeStruct((nt, d, blk), F32),
            jax.ShapeDtypeStruct((nt, d, blk), BF16),
            jax.ShapeDtypeStruct((nt, 1, d), F32),
        ],
        compiler_params=_params(1),
        name="qkv_prompt",
    )(x, wk, wv, wqt, wvt)


def _linear_body(x_ref, w_ref, y_ref):
    y_ref[...] = jnp.dot(x_ref[...].astype(BF16), w_ref[...], preferred_element_type=F32)


def _linear(x, w, tn):
    m, d = x.shape
    n = w.shape[1]
    return pl.pallas_call(
        _linear_body,
        grid=(n // tn,),
        in_specs=[pl.BlockSpec((m, d), lambda j: (0, 0)), pl.BlockSpec((d, tn), lambda j: (0, j))],
        out_specs=pl.BlockSpec((m, tn), lambda j: (0, j)),
        out_shape=jax.ShapeDtypeStruct((m, n), F32),
        compiler_params=_params(1),
        name="qkv_sample",
    )(x, w)


def _head_rows(qpair, hh, hd):
    prow = lax.broadcasted_iota(jnp.int32, qpair.shape, 0)
    return jnp.where((prow // hd) == hh, qpair, 0.0)


def _moba_prompt_body(slope_ref, qt_ref, kb_ref, vt_ref, km_ref, o_ref, bias_ref, *, hd):
    hp = pl.program_id(1)
    i = pl.program_id(2)
    blk = MOBA_BLOCK
    nb = km_ref.shape[1]
    row = lax.broadcasted_iota(jnp.int32, (blk, blk), 0)
    col = lax.broadcasted_iota(jnp.int32, (blk, blk), 1)
    rowf = row.astype(F32)
    bidx = lax.broadcasted_iota(jnp.int32, (nb, blk), 0)
    qpair = qt_ref[0]
    km = km_ref[0]
    outs = []
    for hh in range(2):
        slope = slope_ref[2 * hp + hh]
        q2 = _head_rows(qpair, hh, hd)
        q2b = q2.astype(BF16)
        gate = jnp.dot(km, q2, precision=_HIGHEST, preferred_element_type=F32)
        valid = bidx < i
        g = jnp.where(valid, gate, -jnp.inf)
        rank = jnp.zeros((nb, blk), jnp.int32)
        for m in range(nb):
            gm = g[m:m + 1, :]
            beats = (gm > g) | ((gm == g) & (m < bidx))
            rank = rank + jnp.where(beats, 1, 0)
        sel = valid & (rank < MOBA_TOPK)
        bias_ref[hh] = jnp.where(sel, 0.0, NEG)
        ab = slope * rowf

        kd = kb_ref[pl.ds(pl.multiple_of(i * blk, blk), blk), :]
        s = jnp.dot(kd, q2b, preferred_element_type=F32) + ab
        s = jnp.where(row <= col, s, NEG)
        m0 = jnp.max(s, axis=0, keepdims=True)
        p = jnp.exp(s - m0)
        l0 = jnp.sum(p, axis=0, keepdims=True)
        acc0 = jnp.dot(vt_ref[0, i, hh * hd:(hh + 1) * hd, :], p.astype(BF16),
                       preferred_element_type=F32)

        def body(j, carry, hh=hh, slope=slope, q2b=q2b, ab=ab):
            m_run, l_run, acc = carry
            kj = kb_ref[pl.ds(pl.multiple_of(j * blk, blk), blk), :]
            brow = bias_ref[hh, pl.ds(j, 1), :] + slope * ((j - i) * blk).astype(F32)
            sj = jnp.dot(kj, q2b, preferred_element_type=F32) + ab + brow
            m_new = jnp.maximum(m_run, jnp.max(sj, axis=0, keepdims=True))
            pj = jnp.exp(sj - m_new)
            a = jnp.exp(m_run - m_new)
            l_new = a * l_run + jnp.sum(pj, axis=0, keepdims=True)
            acc_new = a * acc + jnp.dot(vt_ref[0, j, hh * hd:(hh + 1) * hd, :], pj.astype(BF16),
                                        preferred_element_type=F32)
            return m_new, l_new, acc_new

        _, l_fin, acc_fin = lax.fori_loop(0, i, body, (m0, l0, acc0))
        outs.append(acc_fin / l_fin)
    o_ref[...] = jnp.concatenate(outs, axis=0).T.astype(o_ref.dtype)


def _sb_prompt_body(qt_ref, kb_ref, vt_ref, o_ref, *, hd):
    i = pl.program_id(2)
    blk = MOBA_BLOCK
    row = lax.broadcasted_iota(jnp.int32, (blk, blk), 0)
    col = lax.broadcasted_iota(jnp.int32, (blk, blk), 1)
    upper = jnp.where(col > row, 1.0, 0.0).astype(BF16)
    qpair = qt_ref[0]
    outs = []
    for hh in range(2):
        q2b = _head_rows(qpair, hh, hd).astype(BF16)

        def tile(j, r_run, acc, diag, hh=hh, q2b=q2b):
            kj = kb_ref[pl.ds(pl.multiple_of(j * blk, blk), blk), :]
            z = jnp.dot(kj, q2b, preferred_element_type=F32)
            sp = jnp.log1p(jnp.exp(-jnp.abs(z)))
            ls = jnp.minimum(z, 0.0) - sp
            lk = ls - z
            if diag:
                causal = row < col
                lk = jnp.where(causal, lk, 0.0)
            hi = lk.astype(BF16)
            lo = (lk - hi.astype(F32)).astype(BF16)
            suffix = (jnp.dot(upper, hi, preferred_element_type=F32)
                      + jnp.dot(upper, lo, preferred_element_type=F32))
            a = jnp.exp(ls + suffix + r_run)
            if diag:
                a = jnp.where(causal, a, 0.0)
            acc = acc + jnp.dot(vt_ref[0, j, hh * hd:(hh + 1) * hd, :], a.astype(BF16),
                                preferred_element_type=F32)
            return r_run + jnp.sum(lk, axis=0, keepdims=True), acc

        carry = tile(i, jnp.zeros((1, blk), F32), jnp.zeros((hd, blk), F32), True)

        def body(t, c):
            return tile(i - 1 - t, c[0], c[1], False)

        _, acc_fin = lax.fori_loop(0, i, body, carry)
        outs.append(acc_fin)
    o_ref[...] = jnp.concatenate(outs, axis=0).T.astype(o_ref.dtype)


def _prompt_attention(kind, qt, kb, vt, km, slopes, n_batch):
    nt, d, blk = qt.shape
    m = nt * blk
    nq = nt // n_batch
    t = nq * blk
    hd = d // N_HEADS
    pair = 2 * hd
    qt_spec = pl.BlockSpec((1, pair, blk), lambda b, h, i: (b * nq + i, h, 0))
    kb_spec = pl.BlockSpec((t, pair), lambda b, h, i: (b, h))
    vt_spec = pl.BlockSpec((1, nq, pair, blk), lambda b, h, i: (b, 0, h, 0))
    out_spec = pl.BlockSpec((blk, pair), lambda b, h, i: (b * nq + i, h))
    vt4 = vt.reshape(n_batch, nq, d, blk)
    grid = (n_batch, N_HEADS // 2, nq)
    out_shape = jax.ShapeDtypeStruct((m, d), BF16)
    if kind == "moba":
        km_spec = pl.BlockSpec((1, nq, pair), lambda b, h, i: (b, 0, h))
        return pl.pallas_call(
            functools.partial(_moba_prompt_body, hd=hd),
            grid=grid,
            in_specs=[pl.BlockSpec(memory_space=pltpu.SMEM), qt_spec, kb_spec, vt_spec, km_spec],
            out_specs=out_spec,
            out_shape=out_shape,
            scratch_shapes=[pltpu.VMEM((2, nq, blk), F32)],
            compiler_params=_params(3),
            name="moba_prompt",
        )(slopes, qt, kb, vt4, km.reshape(n_batch, nq, d))
    return pl.pallas_call(
        functools.partial(_sb_prompt_body, hd=hd),
        grid=grid,
        in_specs=[qt_spec, kb_spec, vt_spec],
        out_specs=out_spec,
        out_shape=out_shape,
        compiler_params=_params(3),
        name="sb_prompt",
    )(qt, kb, vt4)


def _moba_decode_body(pt_ref, q_ref, kn_ref, vn_ref, slope_ref, k_ref, v_ref, o_ref,
                      m_ref, l_ref, acc_ref, ks_ref, *, n_pages, page):
    del pt_ref
    p = pl.program_id(1)
    ppb = MOBA_BLOCK // page
    nblk = n_pages // ppb
    blk = p // ppb
    q = q_ref[0]
    slope = slope_ref[...]
    kp = k_ref[...]
    vp = v_ref[...]

    @pl.when(p % ppb == 0)
    def _():
        m_ref[blk] = jnp.full(m_ref.shape[1:], NEG, F32)
        l_ref[blk] = jnp.zeros(l_ref.shape[1:], F32)
        acc_ref[blk] = jnp.zeros(acc_ref.shape[1:], F32)
        ks_ref[blk] = jnp.zeros(ks_ref.shape[1:], F32)

    s = jnp.sum(kp * q[None], axis=-1, keepdims=True)
    tpos = (lax.broadcasted_iota(jnp.int32, (page, 1, 1), 0) + p * page).astype(F32)
    s = s + slope[None] * tpos
    m_old = m_ref[blk]
    m_new = jnp.maximum(m_old, jnp.max(s, axis=0))
    a = jnp.exp(m_old - m_new)
    pe = jnp.exp(s - m_new[None])
    l_ref[blk] = a * l_ref[blk] + jnp.sum(pe, axis=0)
    acc_ref[blk] = a * acc_ref[blk] + jnp.sum(pe * vp, axis=0)
    ks_ref[blk] = ks_ref[blk] + jnp.sum(kp, axis=0)
    m_ref[blk] = m_new

    @pl.when(p == n_pages - 1)
    def _():
        kmean = ks_ref[...] * (1.0 / MOBA_BLOCK)
        gate = jnp.sum(kmean * q[None], axis=-1, keepdims=True)
        bidx = lax.broadcasted_iota(jnp.int32, (nblk, 1, 1), 0)
        rank = jnp.zeros(gate.shape, jnp.int32)
        for mm in range(nblk):
            gm = gate[mm:mm + 1]
            beats = (gm > gate) | ((gm == gate) & (mm < bidx))
            rank = rank + jnp.where(beats, 1, 0)
        sel = rank < MOBA_TOPK
        s_own = (jnp.sum(kn_ref[0] * q, axis=-1, keepdims=True)
                 + slope * float(n_pages * page))
        m_blk = m_ref[...]
        m_all = jnp.maximum(s_own, jnp.max(jnp.where(sel, m_blk, NEG), axis=0))
        w = jnp.where(sel, jnp.exp(m_blk - m_all[None]), 0.0)
        w_own = jnp.exp(s_own - m_all)
        l_all = w_own + jnp.sum(w * l_ref[...], axis=0)
        o_all = w_own * vn_ref[0] + jnp.sum(w * acc_ref[...], axis=0)
        o_ref[0] = o_all / l_all


def _sb_decode_body(pt_ref, q_ref, k_ref, v_ref, o_ref, r_ref, acc_ref, lk_ref, suf_ref,
                    *, n_pages, page):
    del pt_ref
    p = pl.program_id(1)

    @pl.when(p == 0)
    def _():
        r_ref[...] = jnp.zeros(r_ref.shape, F32)
        acc_ref[...] = jnp.zeros(acc_ref.shape, F32)

    q = q_ref[0]
    kp = k_ref[...]
    vp = v_ref[...]
    z = jnp.sum(kp * q[None], axis=-1, keepdims=True)
    sp = jnp.log1p(jnp.exp(-jnp.abs(z)))
    ls = jnp.minimum(z, 0.0) - sp
    lk_ref[...] = ls - z

    def body(t, run):
        tt = page - 1 - t
        suf_ref[tt] = run
        return run + lk_ref[tt]

    r_ref[...] = lax.fori_loop(0, page, body, r_ref[...])
    a = jnp.exp(ls + suf_ref[...])
    acc_ref[...] = acc_ref[...] + jnp.sum(a * vp, axis=0)

    @pl.when(p == n_pages - 1)
    def _():
        o_ref[0] = acc_ref[...]


def _decode_attention(kind, layer, q, k_new, v_new, cache_k, cache_v, page_table, slopes):
    n_seq, n_heads, hd = q.shape
    n_pages = page_table.shape[1]
    page = cache_k.shape[2]
    tok_spec = pl.BlockSpec((1, n_heads, hd), lambda b, p, pt: (b, 0, 0))
    out_shape = jax.ShapeDtypeStruct((n_seq, n_heads, hd), F32)
    if kind == "moba":
        assert MOBA_BLOCK % page == 0 and (n_pages * page) % MOBA_BLOCK == 0
        nblk = n_pages * page // MOBA_BLOCK
        page_spec = pl.BlockSpec((None, None, page, n_heads, hd),
                                 lambda b, p, pt: (layer, pt[b, p], 0, 0, 0))
        grid_spec = pltpu.PrefetchScalarGridSpec(
            num_scalar_prefetch=1,
            grid=(n_seq, n_pages),
            in_specs=[tok_spec, tok_spec, tok_spec,
                      pl.BlockSpec((n_heads, 1), lambda b, p, pt: (0, 0)),
                      page_spec, page_spec],
            out_specs=tok_spec,
            scratch_shapes=[pltpu.VMEM((nblk, n_heads, 1), F32),
                            pltpu.VMEM((nblk, n_heads, 1), F32),
                            pltpu.VMEM((nblk, n_heads, hd), F32),
                            pltpu.VMEM((nblk, n_heads, hd), F32)])
        return pl.pallas_call(
            functools.partial(_moba_decode_body, n_pages=n_pages, page=page),
            grid_spec=grid_spec,
            out_shape=out_shape,
            compiler_params=_params(2),
            name="moba_decode",
        )(page_table, q, k_new, v_new, slopes.reshape(n_heads, 1), cache_k, cache_v)
    page_spec = pl.BlockSpec((None, None, page, n_heads, hd),
                             lambda b, p, pt: (layer, pt[b, n_pages - 1 - p], 0, 0, 0))
    grid_spec = pltpu.PrefetchScalarGridSpec(
        num_scalar_prefetch=1,
        grid=(n_seq, n_pages),
        in_specs=[tok_spec, page_spec, page_spec],
        out_specs=tok_spec,
        scratch_shapes=[pltpu.VMEM((n_heads, 1), F32),
                        pltpu.VMEM((n_heads, hd), F32),
                        pltpu.VMEM((page, n_heads, 1), F32),
                        pltpu.VMEM((page, n_heads, 1), F32)])
    return pl.pallas_call(
        functools.partial(_sb_decode_body, n_pages=n_pages, page=page),
        grid_spec=grid_spec,
        out_shape=out_shape,
        compiler_params=_params(2),
        name="sb_decode",
    )(page_table, q, cache_k, cache_v)


def _wo_ln_body(o_ref, x_ref, w_ref, g_ref, b_ref, y_ref, *, alpha):
    y = alpha * x_ref[...] + jnp.dot(o_ref[...].astype(BF16), w_ref[...],
                                     preferred_element_type=F32)
    y_ref[...] = _layer_norm(y, g_ref[...], b_ref[...])


def _wo_ln(o, x, w, g, b, alpha, tm):
    m, d = x.shape
    row = pl.BlockSpec((tm, d), lambda i: (i, 0))
    vec = pl.BlockSpec((1, d), lambda i: (0, 0))
    return pl.pallas_call(
        functools.partial(_wo_ln_body, alpha=alpha),
        grid=(m // tm,),
        in_specs=[row, row, _resident((d, d), lambda i: (0, 0)), vec, vec],
        out_specs=row,
        out_shape=jax.ShapeDtypeStruct((m, d), F32),
        compiler_params=_params(1),
        name="wo_ln",
    )(o, x, w, g, b)


def _ffn_ln_body(x_ref, w1_ref, w3_ref, w2_ref, g_ref, b_ref, y_ref, *, alpha):
    x = x_ref[...]
    xb = x.astype(BF16)
    h1 = jnp.dot(xb, w1_ref[...], preferred_element_type=F32)
    h3 = jnp.dot(xb, w3_ref[...], preferred_element_type=F32)
    h = (_silu(h1) * h3).astype(BF16)
    y = alpha * x + jnp.dot(h, w2_ref[...], preferred_element_type=F32)
    y_ref[...] = _layer_norm(y, g_ref[...], b_ref[...])


def _ffn_ln(x, w1, w3, w2, g, b, alpha, tm):
    m, d = x.shape
    f = w1.shape[1]
    row = pl.BlockSpec((tm, d), lambda i: (i, 0))
    vec = pl.BlockSpec((1, d), lambda i: (0, 0))
    return pl.pallas_call(
        functools.partial(_ffn_ln_body, alpha=alpha),
        grid=(m // tm,),
        in_specs=[row, _resident((d, f), lambda i: (0, 0)), _resident((d, f), lambda i: (0, 0)),
                  _resident((f, d), lambda i: (0, 0)), vec, vec],
        out_specs=row,
        out_shape=jax.ShapeDtypeStruct((m, d), F32),
        compiler_params=_params(1),
        name="ffn_ln",
    )(x, w1, w3, w2, g, b)


def _moe_ln_body(x_ref, wrt_ref, w1_ref, w3_ref, w2_ref, g_ref, b_ref, y_ref,
                 xb_ref, comb_ref, acc_ref, *, alpha, n_experts):
    e = pl.program_id(1)
    c = pl.program_id(2)

    @pl.when((e == 0) & (c == 0))
    def _():
        x = x_ref[...]
        xb_ref[...] = x.astype(BF16)
        logits = lax.dot_general(wrt_ref[...], x, _NT, precision=_HIGHEST,
                                 preferred_element_type=F32)
        eidx = lax.broadcasted_iota(jnp.int32, logits.shape, 0)
        lg = jnp.where(eidx < n_experts, logits, -jnp.inf)
        m1 = jnp.max(lg, axis=0, keepdims=True)
        i1 = jnp.min(jnp.where(lg == m1, eidx, LANES), axis=0, keepdims=True)
        lg2 = jnp.where(eidx == i1, -jnp.inf, lg)
        m2 = jnp.max(lg2, axis=0, keepdims=True)
        i2 = jnp.min(jnp.where(lg2 == m2, eidx, LANES), axis=0, keepdims=True)
        t = jnp.exp(m2 - m1)
        g1 = 1.0 / (1.0 + t)
        g2 = t * g1
        comb = jnp.where(eidx == i1, g1, 0.0) + jnp.where(eidx == i2, g2, 0.0)
        comb_ref[...] = comb.T
        acc_ref[...] = jnp.zeros(acc_ref.shape, F32)

    xb = xb_ref[...]
    lane = lax.broadcasted_iota(jnp.int32, comb_ref.shape, 1)
    ce = jnp.sum(jnp.where(lane == e, comb_ref[...], 0.0), axis=1, keepdims=True)
    h1 = jnp.dot(xb, w1_ref[...].astype(BF16), preferred_element_type=F32)
    h3 = jnp.dot(xb, w3_ref[...].astype(BF16), preferred_element_type=F32)
    h = (_silu(h1) * h3 * ce).astype(BF16)
    acc_ref[...] += jnp.dot(h, w2_ref[...].astype(BF16), preferred_element_type=F32)

    @pl.when((e == n_experts - 1) & (c == pl.num_programs(2) - 1))
    def _():
        y = alpha * x_ref[...] + acc_ref[...]
        y_ref[...] = _layer_norm(y, g_ref[...], b_ref[...])


def _moe_ln(x, wrt, w1, w3, w2, layer, g, b, alpha, tm, fc):
    m, d = x.shape
    n_experts, f = w1.shape[1], w1.shape[3]
    row = pl.BlockSpec((tm, d), lambda i, e, c: (i, 0))
    vec = pl.BlockSpec((1, d), lambda i, e, c: (0, 0))
    return pl.pallas_call(
        functools.partial(_moe_ln_body, alpha=alpha, n_experts=n_experts),
        grid=(m // tm, n_experts, f // fc),
        in_specs=[row,
                  pl.BlockSpec((LANES, d), lambda i, e, c: (0, 0)),
                  pl.BlockSpec((None, None, d, fc), lambda i, e, c: (layer, e, 0, c)),
                  pl.BlockSpec((None, None, d, fc), lambda i, e, c: (layer, e, 0, c)),
                  pl.BlockSpec((None, None, fc, d), lambda i, e, c: (layer, e, c, 0)),
                  vec, vec],
        out_specs=row,
        out_shape=jax.ShapeDtypeStruct((m, d), F32),
        scratch_shapes=[pltpu.VMEM((tm, d), BF16), pltpu.VMEM((tm, LANES), F32),
                        pltpu.VMEM((tm, d), F32)],
        compiler_params=_params(3),
        name="moe_ln",
    )(x, wrt, w1, w3, w2, g, b)


def kernel(x_prompt, x_sample, cache_k, cache_v, page_table, w_qkv, w_o, ln1_g, ln1_b,
           ln2_g, ln2_b, ffn_w1, ffn_w3, ffn_w2, moe_router, moe_w1, moe_w3, moe_w2):
    n_batch, seq, d = x_prompt.shape
    n_seq = x_sample.shape[0]
    depth = w_qkv.shape[0]
    n_heads, hd = cache_k.shape[3], cache_k.shape[4]
    assert n_heads == N_HEADS and n_heads * hd == d and x_sample.shape[1] == 1
    assert seq % MOBA_BLOCK == 0
    n_experts = moe_router.shape[2]
    alpha = (2.0 * depth) ** 0.25
    q_scale = hd ** -0.5
    slopes = jnp.exp2(-8.0 * jnp.arange(1, n_heads + 1, dtype=F32) / n_heads)

    xp = x_prompt.reshape(n_batch * seq, d)
    xs = x_sample.reshape(n_seq, d)
    new_kp, new_vp, new_ks, new_vs = [], [], [], []
    for i in range(depth):
        kind = "moba" if i % 2 == 0 else "sb"
        wq, wk, wv = w_qkv[i, :, :d], w_qkv[i, :, d:2 * d], w_qkv[i, :, 2 * d:]
        kp, vp, kb, qt, vt, km = _qkv_prompt(
            xp, wk.astype(BF16), wv.astype(BF16), wq.T.astype(BF16), wv.T.astype(BF16), q_scale)
        op = _prompt_attention(kind, qt, kb, vt, km, slopes, n_batch)

        qkv_s = _linear(xs, w_qkv[i].astype(BF16), d)
        qs = (qkv_s[:, :d] * q_scale).reshape(n_seq, n_heads, hd)
        ks_ = qkv_s[:, d:2 * d].reshape(n_seq, n_heads, hd)
        vs = qkv_s[:, 2 * d:].reshape(n_seq, n_heads, hd)
        os_ = _decode_attention(kind, i, qs, ks_, vs, cache_k, cache_v, page_table, slopes)

        wo = w_o[i].astype(BF16)
        g1, b1 = ln1_g[i].reshape(1, d), ln1_b[i].reshape(1, d)
        g2, b2 = ln2_g[i].reshape(1, d), ln2_b[i].reshape(1, d)
        xp = _wo_ln(op, xp, wo, g1, b1, alpha, 512)
        xs = _wo_ln(os_.reshape(n_seq, d), xs, wo, g1, b1, alpha, n_seq)
        j = i // 2
        if i % 2 == 0:
            w1, w3, w2 = ffn_w1[j].astype(BF16), ffn_w3[j].astype(BF16), ffn_w2[j].astype(BF16)
            xp = _ffn_ln(xp, w1, w3, w2, g2, b2, alpha, 256)
            xs = _ffn_ln(xs, w1, w3, w2, g2, b2, alpha, n_seq)
        else:
            wrt = jnp.zeros((LANES, d), F32).at[:n_experts].set(moe_router[j].T)
            xp = _moe_ln(xp, wrt, moe_w1, moe_w3, moe_w2, j, g2, b2, alpha, 1024, 512)
            xs = _moe_ln(xs, wrt, moe_w1, moe_w3, moe_w2, j, g2, b2, alpha, n_seq, 512)
        new_kp.append(kp)
        new_vp.append(vp)
        new_ks.append(ks_)
        new_vs.append(vs)

    def stack_prompt(parts):
        return jnp.stack(parts).reshape(depth, n_batch, seq, n_heads, hd)

    def stack_sample(parts):
        return jnp.stack(parts).reshape(depth, n_seq, 1, n_heads, hd)

    return (xp.reshape(n_batch, seq, d), xs.reshape(n_seq, 1, d),
            stack_prompt(new_kp), stack_prompt(new_vp), stack_sample(new_ks), stack_sample(new_vs))
```

```python
import functools
import math

import jax
import jax.numpy as jnp
from jax import lax
from jax.experimental import pallas as pl
from jax.experimental.pallas import tpu as pltpu

N_HEADS = 16
MOBA_BLOCK = 256
MOBA_TOPK = 3
EXPERT_TOPK = 2
LN_EPS = 1e-5
NEG = -1e30
LANES = 128
SUBLANES = 8
VMEM_LIMIT = 56 * 1024 * 1024
HEAD_GROUP = 4
PAGES_PER_STEP = 4
MOE_TILE = 896
LOG2E = math.log2(math.e)
SB_DEAD = -104.0

F32 = jnp.float32
BF16 = jnp.bfloat16
_NT = (((1,), (1,)), ((), ()))
_HIGHEST = lax.Precision.HIGHEST


def _params(n_grid_dims):
    return pltpu.CompilerParams(
        dimension_semantics=("arbitrary",) * n_grid_dims,
        vmem_limit_bytes=VMEM_LIMIT)


def _resident(block_shape, index_map):
    return pl.BlockSpec(block_shape, index_map, pipeline_mode=pl.Buffered(1))


def _layer_norm(y, g, b):
    mu = jnp.mean(y, axis=-1, keepdims=True)
    yc = y - mu
    var = jnp.mean(yc * yc, axis=-1, keepdims=True)
    return yc * lax.rsqrt(var + LN_EPS) * g + b


def _silu(x):
    return x * jax.nn.sigmoid(x)


def _log_sigmoid_pair(z):
    sp = jnp.log(1.0 + jnp.exp(-jnp.abs(z)))
    ls = jnp.minimum(z, 0.0) - sp
    return ls, ls - z


def _split_bf16(x):
    hi = x.astype(BF16)
    lo = (x - hi.astype(F32)).astype(BF16)
    return hi, lo


def _qkv_prompt_body(x_ref, wk_ref, wv_ref, wqt_ref, wvt_ref,
                     k_ref, v_ref, kb_ref, qt_ref, vt_ref, km_ref, *, q_scale):
    xb = x_ref[...].astype(BF16)
    k = jnp.dot(xb, wk_ref[...], preferred_element_type=F32)
    v = jnp.dot(xb, wv_ref[...], preferred_element_type=F32)
    k_ref[...] = k
    v_ref[...] = v
    kb_ref[...] = k.astype(BF16)
    km_ref[0] = jnp.mean(k, axis=0, keepdims=True)
    qt = lax.dot_general(wqt_ref[...], xb, _NT, preferred_element_type=F32)
    qt_ref[0] = qt * q_scale
    vt = lax.dot_general(wvt_ref[...], xb, _NT, preferred_element_type=F32)
    vt_ref[0] = vt.astype(BF16)


def _qkv_prompt(x, wk, wv, wqt, wvt, q_scale):
    m, d = x.shape
    blk = MOBA_BLOCK
    nt = m // blk
    w_spec = _resident((d, d), lambda i: (0, 0))
    return pl.pallas_call(
        functools.partial(_qkv_prompt_body, q_scale=q_scale),
        grid=(nt,),
        in_specs=[pl.BlockSpec((blk, d), lambda i: (i, 0)), w_spec, w_spec, w_spec, w_spec],
        out_specs=[
            pl.BlockSpec((blk, d), lambda i: (i, 0)),
            pl.BlockSpec((blk, d), lambda i: (i, 0)),
            pl.BlockSpec((blk, d), lambda i: (i, 0)),
            pl.BlockSpec((1, d, blk), lambda i: (i, 0, 0)),
            pl.BlockSpec((1, d, blk), lambda i: (i, 0, 0)),
            pl.BlockSpec((1, 1, d), lambda i: (i, 0, 0)),
        ],
        out_shape=[
            jax.ShapeDtypeStruct((m, d), F32),
            jax.ShapeDtypeStruct((m, d), F32),
            jax.ShapeDtypeStruct((m, d), BF16),
            jax.ShapeDtypeStruct((nt, d, blk), F32),
            jax.ShapeDtypeStruct((nt, d, blk), BF16),
            jax.ShapeDtypeStruct((nt, 1, d), F32),
        ],
        compiler_params=_params(1),
        name="qkv_prompt",
    )(x, wk, wv, wqt, wvt)


def _linear_body(x_ref, w_ref, y_ref):
    y_ref[...] = jnp.dot(x_ref[...].astype(BF16), w_ref[...], preferred_element_type=F32)


def _linear(x, w, tn):
    m, d = x.shape
    n = w.shape[1]
    return pl.pallas_call(
        _linear_body,
        grid=(n // tn,),
        in_specs=[pl.BlockSpec((m, d), lambda j: (0, 0)), pl.BlockSpec((d, tn), lambda j: (0, j))],
        out_specs=pl.BlockSpec((m, tn), lambda j: (0, j)),
        out_shape=jax.ShapeDtypeStruct((m, n), F32),
        compiler_params=_params(1),
        name="qkv_sample",
    )(x, w)


def _head_operands(qt_ref, hh, hd):
    pr, half = divmod(hh, 2)
    pair = 2 * hd
    qpair = qt_ref[0, pr * pair:(pr + 1) * pair, :]
    prow = lax.broadcasted_iota(jnp.int32, qpair.shape, 0)
    return jnp.where((prow // hd) == half, qpair, 0.0)


def _moba_prompt_body(slope_ref, qt_ref, kb_ref, vt_ref, km_ref, o_ref,
                      bias_ref, ab_ref, *, hd):
    hg = pl.program_id(1)
    i = pl.program_id(2)
    blk = MOBA_BLOCK
    pair = 2 * hd
    nb = km_ref.shape[1]
    row = lax.broadcasted_iota(jnp.int32, (blk, blk), 0)
    col = lax.broadcasted_iota(jnp.int32, (blk, blk), 1)
    bidx = lax.broadcasted_iota(jnp.int32, (nb, blk), 0)

    @pl.when(i == 0)
    def _():
        rowf = row.astype(F32)
        for hh in range(HEAD_GROUP):
            ab_ref[hh] = slope_ref[HEAD_GROUP * hg + hh] * rowf

    valid = bidx < i
    q2bs = []
    state = []
    for hh in range(HEAD_GROUP):
        pr = hh // 2
        q2 = _head_operands(qt_ref, hh, hd)
        q2b = q2.astype(BF16)
        q2bs.append(q2b)
        gate = jnp.dot(km_ref[0, :, pr * pair:(pr + 1) * pair], q2, precision=_HIGHEST,
                       preferred_element_type=F32)
        g = jnp.where(valid, gate, -jnp.inf)
        rank = jnp.zeros((nb, blk), jnp.int32)
        for m in range(nb):
            gm = g[m:m + 1, :]
            beats = (gm > g) | ((gm == g) & (m < bidx))
            rank = rank + jnp.where(beats, 1, 0)
        sel = valid & (rank < MOBA_TOPK)
        bias_ref[hh] = jnp.where(sel, 0.0, NEG)

        kd = kb_ref[pl.ds(pl.multiple_of(i * blk, blk), blk), pr * pair:(pr + 1) * pair]
        t = jnp.dot(kd, q2b, preferred_element_type=F32) + ab_ref[hh]
        t = jnp.where(row <= col, t, NEG)
        m0 = jnp.max(t, axis=0, keepdims=True)
        p = jnp.exp2(t - m0)
        state.append((m0, jnp.sum(p, axis=0, keepdims=True),
                      jnp.dot(vt_ref[0, i, hh * hd:(hh + 1) * hd, :], p.astype(BF16),
                              preferred_element_type=F32)))

    def body(j, carry):
        heads = range(HEAD_GROUP)
        kjs = [kb_ref[pl.ds(pl.multiple_of(j * blk, blk), blk), pr * pair:(pr + 1) * pair]
               for pr in range(HEAD_GROUP // 2)]
        ts = [jnp.dot(kjs[hh // 2], q2bs[hh], preferred_element_type=F32) + ab_ref[hh]
              for hh in heads]
        brows = [bias_ref[hh, pl.ds(j, 1), :]
                 + slope_ref[HEAD_GROUP * hg + hh] * ((j - i) * blk).astype(F32) for hh in heads]
        m_news = [jnp.maximum(carry[hh][0], jnp.max(ts[hh], axis=0, keepdims=True) + brows[hh])
                  for hh in heads]
        ps = [jnp.exp2(ts[hh] - (m_news[hh] - brows[hh])) for hh in heads]
        pvs = [jnp.dot(vt_ref[0, j, hh * hd:(hh + 1) * hd, :], ps[hh].astype(BF16),
                       preferred_element_type=F32) for hh in heads]
        out = []
        for hh in heads:
            m_run, l_run, acc = carry[hh]
            a = jnp.exp2(m_run - m_news[hh])
            out.append((m_news[hh], a * l_run + jnp.sum(ps[hh], axis=0, keepdims=True),
                        a * acc + pvs[hh]))
        return tuple(out)

    state = lax.fori_loop(0, i, body, tuple(state))
    outs = [acc / l_run for (_, l_run, acc) in state]
    o_ref[...] = jnp.concatenate(outs, axis=0).T.astype(o_ref.dtype)


def _sb_prompt_body(qt_ref, kb_ref, vt_ref, o_ref, *, hd):
    i = pl.program_id(2)
    blk = MOBA_BLOCK
    pair = 2 * hd
    row = lax.broadcasted_iota(jnp.int32, (blk, blk), 0)
    col = lax.broadcasted_iota(jnp.int32, (blk, blk), 1)
    upper = jnp.where(col > row, 1.0, 0.0).astype(BF16)
    causal = row < col
    q2bs = [_head_operands(qt_ref, hh, hd).astype(BF16) for hh in range(HEAD_GROUP)]

    def tile(j, state, diag):
        heads = range(HEAD_GROUP)
        kjs = [kb_ref[pl.ds(pl.multiple_of(j * blk, blk), blk), pr * pair:(pr + 1) * pair]
               for pr in range(HEAD_GROUP // 2)]
        zs = [jnp.dot(kjs[hh // 2], q2bs[hh], preferred_element_type=F32) for hh in heads]
        out = []
        for hh in heads:
            ls, lk = _log_sigmoid_pair(zs[hh])
            if diag:
                lk = jnp.where(causal, lk, 0.0)
            hi, lo = _split_bf16(lk)
            suffix = (jnp.dot(upper, hi, preferred_element_type=F32)
                      + jnp.dot(upper, lo, preferred_element_type=F32))
            e = ls + suffix
            if not diag:
                e = e + state[hh][0]
            a = jnp.exp(e)
            if diag:
                a = jnp.where(causal, a, 0.0)
            pv = jnp.dot(vt_ref[0, j, hh * hd:(hh + 1) * hd, :], a.astype(BF16),
                         preferred_element_type=F32)
            lk_sum = jnp.sum(lk, axis=0, keepdims=True)
            if diag:
                out.append((lk_sum, pv))
            else:
                out.append((state[hh][0] + lk_sum, state[hh][1] + pv))
        return tuple(out)

    def live(state):
        return functools.reduce(jnp.maximum, [jnp.max(r) for r, _ in state])

    state = tile(i, None, True)

    def cond(c):
        return (c[0] >= 0) & (c[1] > SB_DEAD)

    def body(c):
        new = tile(c[0], c[2], False)
        return c[0] - 1, live(new), new

    _, _, state = lax.while_loop(cond, body, (i - 1, live(state), state))
    o_ref[...] = jnp.concatenate([acc for _, acc in state], axis=0).T.astype(o_ref.dtype)


def _prompt_attention(kind, qt, kb, vt, km, slopes, n_batch):
    nt, d, blk = qt.shape
    m = nt * blk
    nq = nt // n_batch
    t = nq * blk
    hd = d // N_HEADS
    gw = HEAD_GROUP * hd
    qt_spec = pl.BlockSpec((1, gw, blk), lambda b, h, i: (b * nq + i, h, 0))
    kb_spec = pl.BlockSpec((t, gw), lambda b, h, i: (b, h))
    vt_spec = pl.BlockSpec((1, nq, gw, blk), lambda b, h, i: (b, 0, h, 0))
    out_spec = pl.BlockSpec((blk, gw), lambda b, h, i: (b * nq + i, h))
    vt4 = vt.reshape(n_batch, nq, d, blk)
    grid = (n_batch, N_HEADS // HEAD_GROUP, nq)
    out_shape = jax.ShapeDtypeStruct((m, d), BF16)
    if kind == "moba":
        km_spec = pl.BlockSpec((1, nq, gw), lambda b, h, i: (b, 0, h))
        return pl.pallas_call(
            functools.partial(_moba_prompt_body, hd=hd),
            grid=grid,
            in_specs=[pl.BlockSpec(memory_space=pltpu.SMEM), qt_spec, kb_spec, vt_spec, km_spec],
            out_specs=out_spec,
            out_shape=out_shape,
            scratch_shapes=[pltpu.VMEM((HEAD_GROUP, nq, blk), F32),
                            pltpu.VMEM((HEAD_GROUP, blk, blk), F32)],
            compiler_params=_params(3),
            name="moba_prompt",
        )(slopes * LOG2E, qt, kb, vt4, km.reshape(n_batch, nq, d))
    return pl.pallas_call(
        functools.partial(_sb_prompt_body, hd=hd),
        grid=grid,
        in_specs=[qt_spec, kb_spec, vt_spec],
        out_specs=out_spec,
        out_shape=out_shape,
        compiler_params=_params(3),
        name="sb_prompt",
    )(qt, kb, vt4)


def _fill_q_lanes(qt_ref, qb_ref):
    qt = qt_ref[0]
    n_heads = qt.shape[1]
    for h in range(n_heads):
        qb_ref[h] = jnp.broadcast_to(qt[:, h:h + 1], qb_ref.shape[1:])


def _page_scores(k_ref, qb_ref):
    rows = [jnp.sum(k_ref[h] * qb_ref[h], axis=0, keepdims=True) for h in range(k_ref.shape[0])]
    return jnp.concatenate(rows, axis=0)


def _lane_sums(acc):
    ones = jnp.ones((SUBLANES, acc.shape[1]), F32)
    return lax.dot_general(ones, acc, _NT, precision=_HIGHEST, preferred_element_type=F32)[0:1]


def _moba_decode_body(pt_ref, q_ref, qt_ref, kn_ref, vn_ref, slope_ref, *refs,
                      n_pages, page, pps):
    del pt_ref
    k_refs, v_refs = refs[:pps], refs[pps:2 * pps]
    o_ref, qb_ref, m_ref, l_ref, gs_ref, acc_ref = refs[2 * pps:]
    st = pl.program_id(1)
    n_heads = q_ref.shape[1]
    ppb = MOBA_BLOCK // page
    nblk = n_pages // ppb
    slope = slope_ref[...]
    lane = lax.broadcasted_iota(jnp.int32, (1, page), 1)

    @pl.when(st == 0)
    def _():
        _fill_q_lanes(qt_ref, qb_ref)

    for bb in range(pps // ppb):
        blk = st * (pps // ppb) + bb
        raws = [_page_scores(k_refs[bb * ppb + g], qb_ref) for g in range(ppb)]
        gs_ref[blk] = sum(jnp.sum(r, axis=1, keepdims=True) for r in raws)
        ss = []
        for g in range(ppb):
            tpos = ((st * pps + bb * ppb + g) * page + lane).astype(F32)
            ss.append(raws[g] + slope * tpos)
        m = functools.reduce(jnp.maximum, [jnp.max(s, axis=1, keepdims=True) for s in ss])
        ps = [jnp.exp(s - m) for s in ss]
        m_ref[blk] = m
        l_ref[blk] = sum(jnp.sum(p, axis=1, keepdims=True) for p in ps)
        for h in range(n_heads):
            acc_ref[blk, h] = sum(ps[g][h:h + 1, :] * v_refs[bb * ppb + g][h] for g in range(ppb))

    @pl.when(st == pl.num_programs(1) - 1)
    def _():
        gate = gs_ref[...]
        bidx = lax.broadcasted_iota(jnp.int32, (nblk, 1, 1), 0)
        rank = jnp.zeros(gate.shape, jnp.int32)
        for mm in range(nblk):
            gm = gate[mm:mm + 1]
            beats = (gm > gate) | ((gm == gate) & (mm < bidx))
            rank = rank + jnp.where(beats, 1, 0)
        sel = rank < MOBA_TOPK
        s_own = (jnp.sum(kn_ref[0] * q_ref[0], axis=-1, keepdims=True)
                 + slope * float(n_pages * page))
        m_blk = m_ref[...]
        m_all = jnp.maximum(s_own, jnp.max(jnp.where(sel, m_blk, NEG), axis=0))
        w = jnp.where(sel, jnp.exp(m_blk - m_all[None]), 0.0)
        w_own = jnp.exp(s_own - m_all)
        l_all = w_own + jnp.sum(w * l_ref[...], axis=0)
        rows = []
        for h in range(n_heads):
            mixed = sum(w[n, h:h + 1, :] * acc_ref[n, h] for n in range(nblk))
            rows.append(_lane_sums(mixed))
        o_past = jnp.concatenate(rows, axis=0)
        o_ref[0] = (w_own * vn_ref[0] + o_past) / l_all


def _sb_decode_body(pt_ref, qt_ref, *refs, page, pps):
    del pt_ref
    k_refs, v_refs = refs[:pps], refs[pps:2 * pps]
    o_ref, qb_ref, r_ref, acc_ref = refs[2 * pps:]
    st = pl.program_id(1)
    n_heads = acc_ref.shape[0]

    @pl.when(st == 0)
    def _():
        _fill_q_lanes(qt_ref, qb_ref)
        r_ref[...] = jnp.zeros(r_ref.shape, F32)
        acc_ref[...] = jnp.zeros(acc_ref.shape, F32)

    src = lax.broadcasted_iota(jnp.int32, (page, page), 0)
    dst = lax.broadcasted_iota(jnp.int32, (page, page), 1)
    later = jnp.where(src > dst, 1.0, 0.0).astype(BF16)
    for g in range(pps):
        z = _page_scores(k_refs[g], qb_ref)
        ls, lk = _log_sigmoid_pair(z)
        hi, lo = _split_bf16(lk)
        suffix = (jnp.dot(hi, later, preferred_element_type=F32)
                  + jnp.dot(lo, later, preferred_element_type=F32))
        r_run = r_ref[...]
        a = jnp.exp(ls + suffix + r_run)
        for h in range(n_heads):
            acc_ref[h] = acc_ref[h] + a[h:h + 1, :] * v_refs[g][h]
        r_ref[...] = r_run + jnp.sum(lk, axis=1, keepdims=True)

    @pl.when(st == pl.num_programs(1) - 1)
    def _():
        o_ref[0] = jnp.concatenate([_lane_sums(acc_ref[h]) for h in range(n_heads)], axis=0)


def _decode_attention(kind, layer, q, k_new, v_new, cache_kt, cache_vt, page_table, slopes):
    n_seq, n_heads, hd = q.shape
    n_pages = page_table.shape[1]
    page = cache_kt.shape[4]
    pps = PAGES_PER_STEP
    assert n_pages % pps == 0
    n_steps = n_pages // pps
    qt = jnp.swapaxes(q, 1, 2)
    tok_spec = pl.BlockSpec((1, n_heads, hd), lambda b, s, pt: (b, 0, 0))
    qt_spec = pl.BlockSpec((1, hd, n_heads), lambda b, s, pt: (b, 0, 0))
    out_shape = jax.ShapeDtypeStruct((n_seq, n_heads, hd), F32)
    qb = pltpu.VMEM((n_heads, hd, page), F32)

    def page_spec(page_of_step):
        return pl.BlockSpec((None, None, n_heads, hd, page),
                            lambda b, s, pt: (layer, pt[b, page_of_step(s)], 0, 0, 0))

    if kind == "moba":
        assert MOBA_BLOCK % page == 0 and pps % (MOBA_BLOCK // page) == 0
        nblk = n_pages * page // MOBA_BLOCK
        pages = [page_spec(lambda s, g=g: s * pps + g) for g in range(pps)]
        stat = pltpu.VMEM((nblk, n_heads, 1), F32)
        grid_spec = pltpu.PrefetchScalarGridSpec(
            num_scalar_prefetch=1,
            grid=(n_seq, n_steps),
            in_specs=[tok_spec, qt_spec, tok_spec, tok_spec,
                      pl.BlockSpec((n_heads, 1), lambda b, s, pt: (0, 0))] + pages + pages,
            out_specs=tok_spec,
            scratch_shapes=[qb, stat, stat, stat, pltpu.VMEM((nblk, n_heads, hd, page), F32)])
        return pl.pallas_call(
            functools.partial(_moba_decode_body, n_pages=n_pages, page=page, pps=pps),
            grid_spec=grid_spec,
            out_shape=out_shape,
            compiler_params=_params(2),
            name="moba_decode",
        )(page_table, q, qt, k_new, v_new, slopes.reshape(n_heads, 1),
          *([cache_kt] * pps), *([cache_vt] * pps))
    pages = [page_spec(lambda s, g=g: n_pages - 1 - (s * pps + g)) for g in range(pps)]
    grid_spec = pltpu.PrefetchScalarGridSpec(
        num_scalar_prefetch=1,
        grid=(n_seq, n_steps),
        in_specs=[qt_spec] + pages + pages,
        out_specs=tok_spec,
        scratch_shapes=[qb, pltpu.VMEM((n_heads, 1), F32), pltpu.VMEM((n_heads, hd, page), F32)])
    return pl.pallas_call(
        functools.partial(_sb_decode_body, page=page, pps=pps),
        grid_spec=grid_spec,
        out_shape=out_shape,
        compiler_params=_params(2),
        name="sb_decode",
    )(page_table, qt, *([cache_kt] * pps), *([cache_vt] * pps))


def _wo_ln_body(o_ref, x_ref, w_ref, g_ref, b_ref, y_ref, *, alpha):
    y = alpha * x_ref[...] + jnp.dot(o_ref[...].astype(BF16), w_ref[...],
                                     preferred_element_type=F32)
    y_ref[...] = _layer_norm(y, g_ref[...], b_ref[...])


def _wo_ln(o, x, w, g, b, alpha, tm):
    m, d = x.shape
    row = pl.BlockSpec((tm, d), lambda i: (i, 0))
    vec = pl.BlockSpec((1, d), lambda i: (0, 0))
    return pl.pallas_call(
        functools.partial(_wo_ln_body, alpha=alpha),
        grid=(m // tm,),
        in_specs=[row, row, _resident((d, d), lambda i: (0, 0)), vec, vec],
        out_specs=row,
        out_shape=jax.ShapeDtypeStruct((m, d), F32),
        compiler_params=_params(1),
        name="wo_ln",
    )(o, x, w, g, b)


def _ffn_ln_body(x_ref, w1_ref, w3_ref, w2_ref, g_ref, b_ref, y_ref, *, alpha):
    x = x_ref[...]
    xb = x.astype(BF16)
    h1 = jnp.dot(xb, w1_ref[...], preferred_element_type=F32)
    h3 = jnp.dot(xb, w3_ref[...], preferred_element_type=F32)
    h = (_silu(h1) * h3).astype(BF16)
    y = alpha * x + jnp.dot(h, w2_ref[...], preferred_element_type=F32)
    y_ref[...] = _layer_norm(y, g_ref[...], b_ref[...])


def _ffn_ln(x, w1, w3, w2, g, b, alpha, tm):
    m, d = x.shape
    f = w1.shape[1]
    row = pl.BlockSpec((tm, d), lambda i: (i, 0))
    vec = pl.BlockSpec((1, d), lambda i: (0, 0))
    return pl.pallas_call(
        functools.partial(_ffn_ln_body, alpha=alpha),
        grid=(m // tm,),
        in_specs=[row, _resident((d, f), lambda i: (0, 0)), _resident((d, f), lambda i: (0, 0)),
                  _resident((f, d), lambda i: (0, 0)), vec, vec],
        out_specs=row,
        out_shape=jax.ShapeDtypeStruct((m, d), F32),
        compiler_params=_params(1),
        name="ffn_ln",
    )(x, w1, w3, w2, g, b)


def _moe_ln_body(x_ref, wrt_ref, w1_ref, w3_ref, w2_ref, g_ref, b_ref, y_ref,
                 xb_ref, rank_ref, combt_ref, rankt_ref, xg_ref, acc_ref, yacc_ref,
                 *, alpha, n_experts, n_rows, ch):
    i = pl.program_id(0)
    e = pl.program_id(1)
    c = pl.program_id(2)
    last_c = pl.num_programs(2) - 1
    tm = x_ref.shape[0]

    @pl.when((e == 0) & (c == 0))
    def _route():
        rows_left = n_rows - i * tm
        rid = lax.broadcasted_iota(jnp.int32, (tm, 1), 0)
        x = jnp.where(rid < rows_left, x_ref[...], 0.0)
        xb_ref[...] = x.astype(BF16)
        logits = lax.dot_general(wrt_ref[...], x, _NT, precision=_HIGHEST,
                                 preferred_element_type=F32)
        eidx = lax.broadcasted_iota(jnp.int32, logits.shape, 0)
        lg = jnp.where(eidx < n_experts, logits, -jnp.inf)
        m1 = jnp.max(lg, axis=0, keepdims=True)
        i1 = jnp.min(jnp.where(lg == m1, eidx, LANES), axis=0, keepdims=True)
        lg2 = jnp.where(eidx == i1, -jnp.inf, lg)
        m2 = jnp.max(lg2, axis=0, keepdims=True)
        i2 = jnp.min(jnp.where(lg2 == m2, eidx, LANES), axis=0, keepdims=True)
        t = jnp.exp(m2 - m1)
        g1 = 1.0 / (1.0 + t)
        g2 = t * g1
        comb = jnp.where(eidx == i1, g1, 0.0) + jnp.where(eidx == i2, g2, 0.0)
        tid = lax.broadcasted_iota(jnp.int32, (1, tm), 1)
        comb = jnp.where(tid < rows_left, comb, 0.0)
        routed = comb > 0.0
        src = lax.broadcasted_iota(jnp.int32, (tm, tm), 0)
        dst = lax.broadcasted_iota(jnp.int32, (tm, tm), 1)
        before = jnp.where(src < dst, 1.0, 0.0).astype(BF16)
        count = jnp.dot(jnp.where(routed, 1.0, 0.0).astype(BF16), before,
                        preferred_element_type=F32)
        rank = jnp.where(routed, count, -1.0)
        rank_ref[...] = rank
        combt_ref[...] = comb.T
        rankt_ref[...] = rank.T
        yacc_ref[...] = jnp.zeros(yacc_ref.shape, F32)

    rank_e = rank_ref[pl.ds(e, 1), :]
    n_e = jnp.sum(jnp.where(rank_e >= 0.0, 1, 0))
    n_chunks = (n_e + ch - 1) // ch

    def chunk_rows(r):
        return pl.ds(pl.multiple_of(r * ch, ch), ch)

    @pl.when(c == 0)
    def _gather():
        def body(r, carry):
            want = (lax.broadcasted_iota(jnp.int32, (ch, tm), 0) + r * ch).astype(F32)
            pick = jnp.where(rank_e == want, 1.0, 0.0).astype(BF16)
            xg_ref[chunk_rows(r), :] = jnp.dot(
                pick, xb_ref[...], preferred_element_type=F32).astype(BF16)
            return carry
        lax.fori_loop(0, n_chunks, body, 0)

    def expert_chunk(r, first):
        xg = xg_ref[chunk_rows(r), :]
        h1 = jnp.dot(xg, w1_ref[...], preferred_element_type=F32)
        h3 = jnp.dot(xg, w3_ref[...], preferred_element_type=F32)
        part = jnp.dot((_silu(h1) * h3).astype(BF16), w2_ref[...], preferred_element_type=F32)
        if first:
            acc_ref[chunk_rows(r), :] = part
        else:
            acc_ref[chunk_rows(r), :] += part

    @pl.when(c == 0)
    def _first_chunk():
        def body(r, carry):
            expert_chunk(r, True)
            return carry
        lax.fori_loop(0, n_chunks, body, 0)

    @pl.when(c > 0)
    def _later_chunk():
        def body(r, carry):
            expert_chunk(r, False)
            return carry
        lax.fori_loop(0, n_chunks, body, 0)

    @pl.when(c == last_c)
    def _scatter():
        lane = lax.broadcasted_iota(jnp.int32, (tm, LANES), 1)
        rank_col = jnp.sum(jnp.where(lane == e, rankt_ref[...], 0.0), axis=1, keepdims=True)
        gate_col = jnp.sum(jnp.where(lane == e, combt_ref[...], 0.0), axis=1, keepdims=True)

        def body(r, carry):
            want = (lax.broadcasted_iota(jnp.int32, (tm, ch), 1) + r * ch).astype(F32)
            place = jnp.where(rank_col == want, 1.0, 0.0).astype(BF16)
            yacc_ref[...] += gate_col * jnp.dot(
                place, acc_ref[chunk_rows(r), :].astype(BF16), preferred_element_type=F32)
            return carry
        lax.fori_loop(0, n_chunks, body, 0)

    @pl.when((e == n_experts - 1) & (c == last_c))
    def _finish():
        y = alpha * x_ref[...] + yacc_ref[...]
        y_ref[...] = _layer_norm(y, g_ref[...], b_ref[...])


def _moe_ln(x, wrt, w1, w3, w2, layer, g, b, alpha, tm, fc):
    m, d = x.shape
    n_experts, f = w1.shape[1], w1.shape[3]
    ch = min(tm, 2 * LANES)
    packed = pl.cdiv(tm, ch) * ch
    row = pl.BlockSpec((tm, d), lambda i, e, c: (i, 0))
    vec = pl.BlockSpec((1, d), lambda i, e, c: (0, 0))
    return pl.pallas_call(
        functools.partial(_moe_ln_body, alpha=alpha, n_experts=n_experts, n_rows=m, ch=ch),
        grid=(pl.cdiv(m, tm), n_experts, f // fc),
        in_specs=[row,
                  pl.BlockSpec((LANES, d), lambda i, e, c: (0, 0)),
                  pl.BlockSpec((None, None, d, fc), lambda i, e, c: (layer, e, 0, c)),
                  pl.BlockSpec((None, None, d, fc), lambda i, e, c: (layer, e, 0, c)),
                  pl.BlockSpec((None, None, fc, d), lambda i, e, c: (layer, e, c, 0)),
                  vec, vec],
        out_specs=row,
        out_shape=jax.ShapeDtypeStruct((m, d), F32),
        scratch_shapes=[pltpu.VMEM((tm, d), BF16),
                        pltpu.VMEM((LANES, tm), F32),
                        pltpu.VMEM((tm, LANES), F32),
                        pltpu.VMEM((tm, LANES), F32),
                        pltpu.VMEM((packed, d), BF16),
                        pltpu.VMEM((packed, d), F32),
                        pltpu.VMEM((tm, d), F32)],
        compiler_params=_params(3),
        name="moe_ln",
    )(x, wrt, w1, w3, w2, g, b)


def kernel(x_prompt, x_sample, cache_k, cache_v, page_table, w_qkv, w_o, ln1_g, ln1_b,
           ln2_g, ln2_b, ffn_w1, ffn_w3, ffn_w2, moe_router, moe_w1, moe_w3, moe_w2):
    n_batch, seq, d = x_prompt.shape
    n_seq = x_sample.shape[0]
    depth = w_qkv.shape[0]
    n_heads, hd = cache_k.shape[3], cache_k.shape[4]
    assert n_heads == N_HEADS and n_heads * hd == d and x_sample.shape[1] == 1
    assert seq % MOBA_BLOCK == 0 and n_heads % HEAD_GROUP == 0
    n_experts = moe_router.shape[2]
    alpha = (2.0 * depth) ** 0.25
    q_scale = hd ** -0.5
    slopes = jnp.exp2(-8.0 * jnp.arange(1, n_heads + 1, dtype=F32) / n_heads)
    cache_kt = jnp.transpose(cache_k, (0, 1, 3, 4, 2))
    cache_vt = jnp.transpose(cache_v, (0, 1, 3, 4, 2))
    mw1, mw3, mw2 = moe_w1.astype(BF16), moe_w3.astype(BF16), moe_w2.astype(BF16)

    xp = x_prompt.reshape(n_batch * seq, d)
    xs = x_sample.reshape(n_seq, d)
    new_kp, new_vp, new_ks, new_vs = [], [], [], []
    for i in range(depth):
        kind = "moba" if i % 2 == 0 else "sb"
        wq, wk, wv = w_qkv[i, :, :d], w_qkv[i, :, d:2 * d], w_qkv[i, :, 2 * d:]
        q_scale_prompt = q_scale * LOG2E if kind == "moba" else q_scale
        kp, vp, kb, qt, vt, km = _qkv_prompt(
            xp, wk.astype(BF16), wv.astype(BF16), wq.T.astype(BF16), wv.T.astype(BF16),
            q_scale_prompt)
        op = _prompt_attention(kind, qt, kb, vt, km, slopes, n_batch)

        qkv_s = _linear(xs, w_qkv[i].astype(BF16), d)
        qs = (qkv_s[:, :d] * q_scale).reshape(n_seq, n_heads, hd)
        ks_ = qkv_s[:, d:2 * d].reshape(n_seq, n_heads, hd)
        vs = qkv_s[:, 2 * d:].reshape(n_seq, n_heads, hd)
        os_ = _decode_attention(kind, i, qs, ks_, vs, cache_kt, cache_vt, page_table, slopes)

        wo = w_o[i].astype(BF16)
        g1, b1 = ln1_g[i].reshape(1, d), ln1_b[i].reshape(1, d)
        g2, b2 = ln2_g[i].reshape(1, d), ln2_b[i].reshape(1, d)
        xp = _wo_ln(op, xp, wo, g1, b1, alpha, 512)
        xs = _wo_ln(os_.reshape(n_seq, d), xs, wo, g1, b1, alpha, n_seq)
        j = i // 2
        if i % 2 == 0:
            w1, w3, w2 = ffn_w1[j].astype(BF16), ffn_w3[j].astype(BF16), ffn_w2[j].astype(BF16)
            xp = _ffn_ln(xp, w1, w3, w2, g2, b2, alpha, 256)
            xs = _ffn_ln(xs, w1, w3, w2, g2, b2, alpha, n_seq)
        else:
            wrt = jnp.zeros((LANES, d), F32).at[:n_experts].set(moe_router[j].T)
            xp = _moe_ln(xp, wrt, mw1, mw3, mw2, j, g2, b2, alpha, MOE_TILE, 512)
            xs = _moe_ln(xs, wrt, mw1, mw3, mw2, j, g2, b2, alpha, n_seq, 512)
        new_kp.append(kp)
        new_vp.append(vp)
        new_ks.append(ks_)
        new_vs.append(vs)

    def stack_prompt(parts):
        return jnp.stack(parts).reshape(depth, n_batch, seq, n_heads, hd)

    def stack_sample(parts):
        return jnp.stack(parts).reshape(depth, n_seq, 1, n_heads, hd)

    return (xp.reshape(n_batch, seq, d), xs.reshape(n_seq, 1, d),
            stack_prompt(new_kp), stack_prompt(new_vp), stack_sample(new_ks), stack_sample(new_vs))
```

```python
import functools
import math

import jax
import jax.numpy as jnp
from jax import lax
from jax.experimental import pallas as pl
from jax.experimental.pallas import tpu as pltpu

N_HEADS = 16
MOBA_BLOCK = 256
MOBA_TOPK = 3
EXPERT_TOPK = 2
LN_EPS = 1e-5
NEG = -1e30
LANES = 128
SUBLANES = 8
VMEM_LIMIT = 56 * 1024 * 1024
HEAD_GROUP = 8
PAGES_PER_STEP = 4
MOE_TILE = 896
MOE_FF_CHUNK = 896
LOG2E = math.log2(math.e)
SB_DEAD = -104.0

F32 = jnp.float32
BF16 = jnp.bfloat16
_NT = (((1,), (1,)), ((), ()))
_HIGHEST = lax.Precision.HIGHEST


def _params(n_grid_dims):
    return pltpu.CompilerParams(
        dimension_semantics=("arbitrary",) * n_grid_dims,
        vmem_limit_bytes=VMEM_LIMIT)


def _resident(block_shape, index_map):
    return pl.BlockSpec(block_shape, index_map, pipeline_mode=pl.Buffered(1))


def _layer_norm(y, g, b):
    mu = jnp.mean(y, axis=-1, keepdims=True)
    yc = y - mu
    var = jnp.mean(yc * yc, axis=-1, keepdims=True)
    return yc * lax.rsqrt(var + LN_EPS) * g + b


def _silu(x):
    return x * jax.nn.sigmoid(x)


def _log_sigmoid_pair(z):
    sp = jnp.log(1.0 + jnp.exp(-jnp.abs(z)))
    ls = jnp.minimum(z, 0.0) - sp
    return ls, ls - z


def _split_bf16(x):
    hi = x.astype(BF16)
    lo = (x - hi.astype(F32)).astype(BF16)
    return hi, lo


def _qkv_prompt_body(x_ref, wk_ref, wqt_ref, wkt_ref, wvt_ref,
                     kt_ref, vt_ref, kb_ref, qt_ref, vtb_ref, km_ref, *, q_scale):
    xb = x_ref[...].astype(BF16)
    k = jnp.dot(xb, wk_ref[...], preferred_element_type=F32)
    kb_ref[...] = k.astype(BF16)
    km_ref[0] = jnp.mean(k, axis=0, keepdims=True)
    qt = lax.dot_general(wqt_ref[...], xb, _NT, preferred_element_type=F32)
    qt_ref[0] = qt * q_scale
    kt_ref[0] = lax.dot_general(wkt_ref[...], xb, _NT, preferred_element_type=F32)
    vt = lax.dot_general(wvt_ref[...], xb, _NT, preferred_element_type=F32)
    vt_ref[0] = vt
    vtb_ref[0] = vt.astype(BF16)


def _qkv_prompt(x, wk, wqt, wkt, wvt, q_scale, n_batch):
    m, d = x.shape
    blk = MOBA_BLOCK
    nt = m // blk
    nq = nt // n_batch
    w_spec = _resident((d, d), lambda i: (0, 0))
    seq_major = pl.BlockSpec((1, d, blk), lambda i: (i // nq, 0, i % nq))
    return pl.pallas_call(
        functools.partial(_qkv_prompt_body, q_scale=q_scale),
        grid=(nt,),
        in_specs=[pl.BlockSpec((blk, d), lambda i: (i, 0)), w_spec, w_spec, w_spec, w_spec],
        out_specs=[
            seq_major,
            seq_major,
            pl.BlockSpec((blk, d), lambda i: (i, 0)),
            pl.BlockSpec((1, d, blk), lambda i: (i, 0, 0)),
            pl.BlockSpec((1, d, blk), lambda i: (i, 0, 0)),
            pl.BlockSpec((1, 1, d), lambda i: (i, 0, 0)),
        ],
        out_shape=[
            jax.ShapeDtypeStruct((n_batch, d, nq * blk), F32),
            jax.ShapeDtypeStruct((n_batch, d, nq * blk), F32),
            jax.ShapeDtypeStruct((m, d), BF16),
            jax.ShapeDtypeStruct((nt, d, blk), F32),
            jax.ShapeDtypeStruct((nt, d, blk), BF16),
            jax.ShapeDtypeStruct((nt, 1, d), F32),
        ],
        compiler_params=_params(1),
        name="qkv_prompt",
    )(x, wk, wqt, wkt, wvt)


def _linear_body(x_ref, w_ref, y_ref):
    y_ref[...] = jnp.dot(x_ref[...].astype(BF16), w_ref[...], preferred_element_type=F32)


def _linear(x, w, tn):
    m, d = x.shape
    n = w.shape[1]
    return pl.pallas_call(
        _linear_body,
        grid=(n // tn,),
        in_specs=[pl.BlockSpec((m, d), lambda j: (0, 0)), pl.BlockSpec((d, tn), lambda j: (0, j))],
        out_specs=pl.BlockSpec((m, tn), lambda j: (0, j)),
        out_shape=jax.ShapeDtypeStruct((m, n), F32),
        compiler_params=_params(1),
        name="qkv_sample",
    )(x, w)


def _head_operands(qt_ref, hh, hd):
    pr, half = divmod(hh, 2)
    pair = 2 * hd
    qpair = qt_ref[0, pr * pair:(pr + 1) * pair, :]
    prow = lax.broadcasted_iota(jnp.int32, qpair.shape, 0)
    return jnp.where((prow // hd) == half, qpair, 0.0)


def _moba_prompt_body(slope_ref, qt_ref, kb_ref, vt_ref, km_ref, o_ref,
                      bias_ref, ab_ref, *, hd):
    hg = pl.program_id(1)
    i = pl.program_id(2)
    blk = MOBA_BLOCK
    pair = 2 * hd
    nb = km_ref.shape[1]
    row = lax.broadcasted_iota(jnp.int32, (blk, blk), 0)
    col = lax.broadcasted_iota(jnp.int32, (blk, blk), 1)
    bidx = lax.broadcasted_iota(jnp.int32, (nb, blk), 0)

    @pl.when(i == 0)
    def _():
        rowf = row.astype(F32)
        for hh in range(HEAD_GROUP):
            ab_ref[hh] = slope_ref[HEAD_GROUP * hg + hh] * rowf

    valid = bidx < i
    q2bs = []
    state = []
    for hh in range(HEAD_GROUP):
        pr = hh // 2
        q2 = _head_operands(qt_ref, hh, hd)
        q2b = q2.astype(BF16)
        q2bs.append(q2b)
        gate = jnp.dot(km_ref[0, :, pr * pair:(pr + 1) * pair], q2, precision=_HIGHEST,
                       preferred_element_type=F32)
        g = jnp.where(valid, gate, -jnp.inf)
        rank = jnp.zeros((nb, blk), jnp.int32)
        for m in range(nb):
            gm = g[m:m + 1, :]
            beats = (gm > g) | ((gm == g) & (m < bidx))
            rank = rank + jnp.where(beats, 1, 0)
        sel = valid & (rank < MOBA_TOPK)
        bias_ref[hh] = jnp.where(sel, 0.0, NEG)

        kd = kb_ref[pl.ds(pl.multiple_of(i * blk, blk), blk), pr * pair:(pr + 1) * pair]
        t = jnp.dot(kd, q2b, preferred_element_type=F32) + ab_ref[hh]
        t = jnp.where(row <= col, t, NEG)
        m0 = jnp.max(t, axis=0, keepdims=True)
        p = jnp.exp2(t - m0)
        state.append((m0, jnp.sum(p, axis=0, keepdims=True),
                      jnp.dot(vt_ref[0, i, hh * hd:(hh + 1) * hd, :], p.astype(BF16),
                              preferred_element_type=F32)))

    def body(j, carry):
        heads = range(HEAD_GROUP)
        kjs = [kb_ref[pl.ds(pl.multiple_of(j * blk, blk), blk), pr * pair:(pr + 1) * pair]
               for pr in range(HEAD_GROUP // 2)]
        ts = [jnp.dot(kjs[hh // 2], q2bs[hh], preferred_element_type=F32) + ab_ref[hh]
              for hh in heads]
        brows = [bias_ref[hh, pl.ds(j, 1), :]
                 + slope_ref[HEAD_GROUP * hg + hh] * ((j - i) * blk).astype(F32) for hh in heads]
        m_news = [jnp.maximum(carry[hh][0], jnp.max(ts[hh], axis=0, keepdims=True) + brows[hh])
                  for hh in heads]
        ps = [jnp.exp2(ts[hh] - (m_news[hh] - brows[hh])) for hh in heads]
        pvs = [jnp.dot(vt_ref[0, j, hh * hd:(hh + 1) * hd, :], ps[hh].astype(BF16),
                       preferred_element_type=F32) for hh in heads]
        out = []
        for hh in heads:
            m_run, l_run, acc = carry[hh]
            a = jnp.exp2(m_run - m_news[hh])
            out.append((m_news[hh], a * l_run + jnp.sum(ps[hh], axis=0, keepdims=True),
                        a * acc + pvs[hh]))
        return tuple(out)

    state = lax.fori_loop(0, i, body, tuple(state))
    outs = [acc / l_run for (_, l_run, acc) in state]
    o_ref[...] = jnp.concatenate(outs, axis=0).T.astype(o_ref.dtype)


def _sb_prompt_body(qt_ref, kb_ref, vt_ref, o_ref, *, hd):
    i = pl.program_id(2)
    blk = MOBA_BLOCK
    pair = 2 * hd
    row = lax.broadcasted_iota(jnp.int32, (blk, blk), 0)
    col = lax.broadcasted_iota(jnp.int32, (blk, blk), 1)
    upper = jnp.where(col > row, 1.0, 0.0).astype(BF16)
    causal = row < col
    q2bs = [_head_operands(qt_ref, hh, hd).astype(BF16) for hh in range(HEAD_GROUP)]

    def tile(j, state, diag):
        heads = range(HEAD_GROUP)
        kjs = [kb_ref[pl.ds(pl.multiple_of(j * blk, blk), blk), pr * pair:(pr + 1) * pair]
               for pr in range(HEAD_GROUP // 2)]
        zs = [jnp.dot(kjs[hh // 2], q2bs[hh], preferred_element_type=F32) for hh in heads]
        out = []
        for hh in heads:
            ls, lk = _log_sigmoid_pair(zs[hh])
            if diag:
                lk = jnp.where(causal, lk, 0.0)
            hi, lo = _split_bf16(lk)
            suffix = (jnp.dot(upper, hi, preferred_element_type=F32)
                      + jnp.dot(upper, lo, preferred_element_type=F32))
            e = ls + suffix
            if not diag:
                e = e + state[hh][0]
            a = jnp.exp(e)
            if diag:
                a = jnp.where(causal, a, 0.0)
            pv = jnp.dot(vt_ref[0, j, hh * hd:(hh + 1) * hd, :], a.astype(BF16),
                         preferred_element_type=F32)
            lk_sum = jnp.sum(lk, axis=0, keepdims=True)
            if diag:
                out.append((lk_sum, pv))
            else:
                out.append((state[hh][0] + lk_sum, state[hh][1] + pv))
        return tuple(out)

    def live(state):
        return functools.reduce(jnp.maximum, [jnp.max(r) for r, _ in state])

    state = tile(i, None, True)

    def cond(c):
        return (c[0] >= 0) & (c[1] > SB_DEAD)

    def body(c):
        new = tile(c[0], c[2], False)
        return c[0] - 1, live(new), new

    _, _, state = lax.while_loop(cond, body, (i - 1, live(state), state))
    o_ref[...] = jnp.concatenate([acc for _, acc in state], axis=0).T.astype(o_ref.dtype)


def _prompt_attention(kind, qt, kb, vt, km, slopes, n_batch):
    nt, d, blk = qt.shape
    m = nt * blk
    nq = nt // n_batch
    t = nq * blk
    hd = d // N_HEADS
    gw = HEAD_GROUP * hd
    qt_spec = pl.BlockSpec((1, gw, blk), lambda b, h, i: (b * nq + i, h, 0))
    kb_spec = pl.BlockSpec((t, gw), lambda b, h, i: (b, h))
    vt_spec = pl.BlockSpec((1, nq, gw, blk), lambda b, h, i: (b, 0, h, 0))
    out_spec = pl.BlockSpec((blk, gw), lambda b, h, i: (b * nq + i, h))
    vt4 = vt.reshape(n_batch, nq, d, blk)
    grid = (n_batch, N_HEADS // HEAD_GROUP, nq)
    out_shape = jax.ShapeDtypeStruct((m, d), BF16)
    if kind == "moba":
        km_spec = pl.BlockSpec((1, nq, gw), lambda b, h, i: (b, 0, h))
        return pl.pallas_call(
            functools.partial(_moba_prompt_body, hd=hd),
            grid=grid,
            in_specs=[pl.BlockSpec(memory_space=pltpu.SMEM), qt_spec, kb_spec, vt_spec, km_spec],
            out_specs=out_spec,
            out_shape=out_shape,
            scratch_shapes=[pltpu.VMEM((HEAD_GROUP, nq, blk), F32),
                            pltpu.VMEM((HEAD_GROUP, blk, blk), F32)],
            compiler_params=_params(3),
            name="moba_prompt",
        )(slopes * LOG2E, qt, kb, vt4, km.reshape(n_batch, nq, d))
    return pl.pallas_call(
        functools.partial(_sb_prompt_body, hd=hd),
        grid=grid,
        in_specs=[qt_spec, kb_spec, vt_spec],
        out_specs=out_spec,
        out_shape=out_shape,
        compiler_params=_params(3),
        name="sb_prompt",
    )(qt, kb, vt4)


def _fill_q_lanes(qt_ref, qb_ref):
    qt = qt_ref[0]
    n_heads = qt.shape[1]
    for h in range(n_heads):
        qb_ref[h] = jnp.broadcast_to(qt[:, h:h + 1], qb_ref.shape[1:])


def _page_scores(k_ref, qb_ref):
    rows = [jnp.sum(k_ref[h] * qb_ref[h], axis=0, keepdims=True) for h in range(k_ref.shape[0])]
    return jnp.concatenate(rows, axis=0)


def _lane_sums(acc):
    ones = jnp.ones((SUBLANES, acc.shape[1]), F32)
    return lax.dot_general(ones, acc, _NT, precision=_HIGHEST, preferred_element_type=F32)[0:1]


def _moba_decode_body(pt_ref, q_ref, qt_ref, kn_ref, vn_ref, slope_ref, *refs,
                      n_pages, page, pps):
    del pt_ref
    k_refs, v_refs = refs[:pps], refs[pps:2 * pps]
    o_ref, qb_ref, m_ref, l_ref, gs_ref, acc_ref = refs[2 * pps:]
    st = pl.program_id(1)
    n_heads = q_ref.shape[1]
    ppb = MOBA_BLOCK // page
    nblk = n_pages // ppb
    slope = slope_ref[...]
    lane = lax.broadcasted_iota(jnp.int32, (1, page), 1)

    @pl.when(st == 0)
    def _():
        _fill_q_lanes(qt_ref, qb_ref)

    for bb in range(pps // ppb):
        blk = st * (pps // ppb) + bb
        raws = [_page_scores(k_refs[bb * ppb + g], qb_ref) for g in range(ppb)]
        gs_ref[blk] = sum(jnp.sum(r, axis=1, keepdims=True) for r in raws)
        ss = []
        for g in range(ppb):
            tpos = ((st * pps + bb * ppb + g) * page + lane).astype(F32)
            ss.append(raws[g] + slope * tpos)
        m = functools.reduce(jnp.maximum, [jnp.max(s, axis=1, keepdims=True) for s in ss])
        ps = [jnp.exp(s - m) for s in ss]
        m_ref[blk] = m
        l_ref[blk] = sum(jnp.sum(p, axis=1, keepdims=True) for p in ps)
        for h in range(n_heads):
            acc_ref[blk, h] = sum(ps[g][h:h + 1, :] * v_refs[bb * ppb + g][h] for g in range(ppb))

    @pl.when(st == pl.num_programs(1) - 1)
    def _():
        gate = gs_ref[...]
        bidx = lax.broadcasted_iota(jnp.int32, (nblk, 1, 1), 0)
        rank = jnp.zeros(gate.shape, jnp.int32)
        for mm in range(nblk):
            gm = gate[mm:mm + 1]
            beats = (gm > gate) | ((gm == gate) & (mm < bidx))
            rank = rank + jnp.where(beats, 1, 0)
        sel = rank < MOBA_TOPK
        s_own = (jnp.sum(kn_ref[0] * q_ref[0], axis=-1, keepdims=True)
                 + slope * float(n_pages * page))
        m_blk = m_ref[...]
        m_all = jnp.maximum(s_own, jnp.max(jnp.where(sel, m_blk, NEG), axis=0))
        w = jnp.where(sel, jnp.exp(m_blk - m_all[None]), 0.0)
        w_own = jnp.exp(s_own - m_all)
        l_all = w_own + jnp.sum(w * l_ref[...], axis=0)
        rows = []
        for h in range(n_heads):
            mixed = sum(w[n, h:h + 1, :] * acc_ref[n, h] for n in range(nblk))
            rows.append(_lane_sums(mixed))
        o_past = jnp.concatenate(rows, axis=0)
        o_ref[0] = (w_own * vn_ref[0] + o_past) / l_all


def _sb_decode_body(pt_ref, qt_ref, *refs, page, pps):
    del pt_ref
    k_refs, v_refs = refs[:pps], refs[pps:2 * pps]
    o_ref, qb_ref, r_ref, acc_ref = refs[2 * pps:]
    st = pl.program_id(1)
    n_heads = acc_ref.shape[0]

    @pl.when(st == 0)
    def _():
        _fill_q_lanes(qt_ref, qb_ref)
        r_ref[...] = jnp.zeros(r_ref.shape, F32)
        acc_ref[...] = jnp.zeros(acc_ref.shape, F32)

    src = lax.broadcasted_iota(jnp.int32, (page, page), 0)
    dst = lax.broadcasted_iota(jnp.int32, (page, page), 1)
    later = jnp.where(src > dst, 1.0, 0.0).astype(BF16)
    for g in range(pps):
        @pl.when(jnp.max(r_ref[...]) > SB_DEAD)
        def _(g=g):
            z = _page_scores(k_refs[g], qb_ref)
            ls, lk = _log_sigmoid_pair(z)
            hi, lo = _split_bf16(lk)
            suffix = (jnp.dot(hi, later, preferred_element_type=F32)
                      + jnp.dot(lo, later, preferred_element_type=F32))
            r_run = r_ref[...]
            a = jnp.exp(ls + suffix + r_run)
            for h in range(n_heads):
                acc_ref[h] = acc_ref[h] + a[h:h + 1, :] * v_refs[g][h]
            r_ref[...] = r_run + jnp.sum(lk, axis=1, keepdims=True)

    @pl.when(st == pl.num_programs(1) - 1)
    def _():
        o_ref[0] = jnp.concatenate([_lane_sums(acc_ref[h]) for h in range(n_heads)], axis=0)


def _decode_attention(kind, layer, q, k_new, v_new, cache_kt, cache_vt, page_table, slopes):
    n_seq, n_heads, hd = q.shape
    n_pages = page_table.shape[1]
    page = cache_kt.shape[4]
    pps = PAGES_PER_STEP
    assert n_pages % pps == 0
    n_steps = n_pages // pps
    qt = jnp.swapaxes(q, 1, 2)
    tok_spec = pl.BlockSpec((1, n_heads, hd), lambda b, s, pt: (b, 0, 0))
    qt_spec = pl.BlockSpec((1, hd, n_heads), lambda b, s, pt: (b, 0, 0))
    out_shape = jax.ShapeDtypeStruct((n_seq, n_heads, hd), F32)
    qb = pltpu.VMEM((n_heads, hd, page), F32)

    def page_spec(page_of_step):
        return pl.BlockSpec((None, None, n_heads, hd, page),
                            lambda b, s, pt: (layer, pt[b, page_of_step(s)], 0, 0, 0))

    if kind == "moba":
        assert MOBA_BLOCK % page == 0 and pps % (MOBA_BLOCK // page) == 0
        nblk = n_pages * page // MOBA_BLOCK
        pages = [page_spec(lambda s, g=g: s * pps + g) for g in range(pps)]
        stat = pltpu.VMEM((nblk, n_heads, 1), F32)
        grid_spec = pltpu.PrefetchScalarGridSpec(
            num_scalar_prefetch=1,
            grid=(n_seq, n_steps),
            in_specs=[tok_spec, qt_spec, tok_spec, tok_spec,
                      pl.BlockSpec((n_heads, 1), lambda b, s, pt: (0, 0))] + pages + pages,
            out_specs=tok_spec,
            scratch_shapes=[qb, stat, stat, stat, pltpu.VMEM((nblk, n_heads, hd, page), F32)])
        return pl.pallas_call(
            functools.partial(_moba_decode_body, n_pages=n_pages, page=page, pps=pps),
            grid_spec=grid_spec,
            out_shape=out_shape,
            compiler_params=_params(2),
            name="moba_decode",
        )(page_table, q, qt, k_new, v_new, slopes.reshape(n_heads, 1),
          *([cache_kt] * pps), *([cache_vt] * pps))
    pages = [page_spec(lambda s, g=g: n_pages - 1 - (s * pps + g)) for g in range(pps)]
    grid_spec = pltpu.PrefetchScalarGridSpec(
        num_scalar_prefetch=1,
        grid=(n_seq, n_steps),
        in_specs=[qt_spec] + pages + pages,
        out_specs=tok_spec,
        scratch_shapes=[qb, pltpu.VMEM((n_heads, 1), F32), pltpu.VMEM((n_heads, hd, page), F32)])
    return pl.pallas_call(
        functools.partial(_sb_decode_body, page=page, pps=pps),
        grid_spec=grid_spec,
        out_shape=out_shape,
        compiler_params=_params(2),
        name="sb_decode",
    )(page_table, qt, *([cache_kt] * pps), *([cache_vt] * pps))


def _wo_ln_body(o_ref, x_ref, w_ref, g_ref, b_ref, y_ref, *, alpha):
    y = alpha * x_ref[...] + jnp.dot(o_ref[...].astype(BF16), w_ref[...],
                                     preferred_element_type=F32)
    y_ref[...] = _layer_norm(y, g_ref[...], b_ref[...])


def _wo_ln(o, x, w, g, b, alpha, tm):
    m, d = x.shape
    row = pl.BlockSpec((tm, d), lambda i: (i, 0))
    vec = pl.BlockSpec((1, d), lambda i: (0, 0))
    return pl.pallas_call(
        functools.partial(_wo_ln_body, alpha=alpha),
        grid=(m // tm,),
        in_specs=[row, row, _resident((d, d), lambda i: (0, 0)), vec, vec],
        out_specs=row,
        out_shape=jax.ShapeDtypeStruct((m, d), F32),
        compiler_params=_params(1),
        name="wo_ln",
    )(o, x, w, g, b)


def _ffn_ln_body(x_ref, w1_ref, w3_ref, w2_ref, g_ref, b_ref, y_ref, *, alpha):
    x = x_ref[...]
    xb = x.astype(BF16)
    h1 = jnp.dot(xb, w1_ref[...], preferred_element_type=F32)
    h3 = jnp.dot(xb, w3_ref[...], preferred_element_type=F32)
    h = (_silu(h1) * h3).astype(BF16)
    y = alpha * x + jnp.dot(h, w2_ref[...], preferred_element_type=F32)
    y_ref[...] = _layer_norm(y, g_ref[...], b_ref[...])


def _ffn_ln(x, w1, w3, w2, g, b, alpha, tm):
    m, d = x.shape
    f = w1.shape[1]
    row = pl.BlockSpec((tm, d), lambda i: (i, 0))
    vec = pl.BlockSpec((1, d), lambda i: (0, 0))
    return pl.pallas_call(
        functools.partial(_ffn_ln_body, alpha=alpha),
        grid=(m // tm,),
        in_specs=[row, _resident((d, f), lambda i: (0, 0)), _resident((d, f), lambda i: (0, 0)),
                  _resident((f, d), lambda i: (0, 0)), vec, vec],
        out_specs=row,
        out_shape=jax.ShapeDtypeStruct((m, d), F32),
        compiler_params=_params(1),
        name="ffn_ln",
    )(x, w1, w3, w2, g, b)


def _moe_ln_body(x_ref, wrt_ref, w1_ref, w3_ref, w2_ref, g_ref, b_ref, y_ref,
                 xb_ref, rank_ref, combt_ref, rankt_ref, xg_ref, acc_ref, yacc_ref,
                 *, alpha, n_experts, n_rows, ch):
    i = pl.program_id(0)
    e = pl.program_id(1)
    c = pl.program_id(2)
    last_c = pl.num_programs(2) - 1
    tm = x_ref.shape[0]

    @pl.when((e == 0) & (c == 0))
    def _route():
        rows_left = n_rows - i * tm
        rid = lax.broadcasted_iota(jnp.int32, (tm, 1), 0)
        x = jnp.where(rid < rows_left, x_ref[...], 0.0)
        xb_ref[...] = x.astype(BF16)
        logits = lax.dot_general(wrt_ref[...], x, _NT, precision=_HIGHEST,
                                 preferred_element_type=F32)
        eidx = lax.broadcasted_iota(jnp.int32, logits.shape, 0)
        lg = jnp.where(eidx < n_experts, logits, -jnp.inf)
        m1 = jnp.max(lg, axis=0, keepdims=True)
        i1 = jnp.min(jnp.where(lg == m1, eidx, LANES), axis=0, keepdims=True)
        lg2 = jnp.where(eidx == i1, -jnp.inf, lg)
        m2 = jnp.max(lg2, axis=0, keepdims=True)
        i2 = jnp.min(jnp.where(lg2 == m2, eidx, LANES), axis=0, keepdims=True)
        t = jnp.exp(m2 - m1)
        g1 = 1.0 / (1.0 + t)
        g2 = t * g1
        comb = jnp.where(eidx == i1, g1, 0.0) + jnp.where(eidx == i2, g2, 0.0)
        tid = lax.broadcasted_iota(jnp.int32, (1, tm), 1)
        comb = jnp.where(tid < rows_left, comb, 0.0)
        routed = comb > 0.0
        src = lax.broadcasted_iota(jnp.int32, (tm, tm), 0)
        dst = lax.broadcasted_iota(jnp.int32, (tm, tm), 1)
        before = jnp.where(src < dst, 1.0, 0.0).astype(BF16)
        count = jnp.dot(jnp.where(routed, 1.0, 0.0).astype(BF16), before,
                        preferred_element_type=F32)
        rank = jnp.where(routed, count, -1.0)
        rank_ref[...] = rank
        combt_ref[...] = comb.T
        rankt_ref[...] = rank.T
        yacc_ref[...] = jnp.zeros(yacc_ref.shape, F32)

    rank_e = rank_ref[pl.ds(e, 1), :]
    n_e = jnp.sum(jnp.where(rank_e >= 0.0, 1, 0))
    n_chunks = (n_e + ch - 1) // ch

    def chunk_rows(r):
        return pl.ds(pl.multiple_of(r * ch, ch), ch)

    @pl.when(c == 0)
    def _gather():
        def body(r, carry):
            want = (lax.broadcasted_iota(jnp.int32, (ch, tm), 0) + r * ch).astype(F32)
            pick = jnp.where(rank_e == want, 1.0, 0.0).astype(BF16)
            xg_ref[chunk_rows(r), :] = jnp.dot(
                pick, xb_ref[...], preferred_element_type=F32).astype(BF16)
            return carry
        lax.fori_loop(0, n_chunks, body, 0)

    def expert_chunk(r, first):
        xg = xg_ref[chunk_rows(r), :]
        h1 = jnp.dot(xg, w1_ref[...], preferred_element_type=F32)
        h3 = jnp.dot(xg, w3_ref[...], preferred_element_type=F32)
        part = jnp.dot((_silu(h1) * h3).astype(BF16), w2_ref[...], preferred_element_type=F32)
        if first:
            acc_ref[chunk_rows(r), :] = part
        else:
            acc_ref[chunk_rows(r), :] += part

    @pl.when(c == 0)
    def _first_chunk():
        def body(r, carry):
            expert_chunk(r, True)
            return carry
        lax.fori_loop(0, n_chunks, body, 0)

    @pl.when(c > 0)
    def _later_chunk():
        def body(r, carry):
            expert_chunk(r, False)
            return carry
        lax.fori_loop(0, n_chunks, body, 0)

    @pl.when(c == last_c)
    def _scatter():
        lane = lax.broadcasted_iota(jnp.int32, (tm, LANES), 1)
        rank_col = jnp.sum(jnp.where(lane == e, rankt_ref[...], 0.0), axis=1, keepdims=True)
        gate_col = jnp.sum(jnp.where(lane == e, combt_ref[...], 0.0), axis=1, keepdims=True)

        def body(r, carry):
            want = (lax.broadcasted_iota(jnp.int32, (tm, ch), 1) + r * ch).astype(F32)
            place = jnp.where(rank_col == want, 1.0, 0.0).astype(BF16)
            yacc_ref[...] += gate_col * jnp.dot(
                place, acc_ref[chunk_rows(r), :].astype(BF16), preferred_element_type=F32)
            return carry
        lax.fori_loop(0, n_chunks, body, 0)

    @pl.when((e == n_experts - 1) & (c == last_c))
    def _finish():
        y = alpha * x_ref[...] + yacc_ref[...]
        y_ref[...] = _layer_norm(y, g_ref[...], b_ref[...])


def _moe_ln(x, wrt, w1, w3, w2, layer, g, b, alpha, tm, fc):
    m, d = x.shape
    n_experts, f = w1.shape[1], w1.shape[3]
    ch = min(tm, 2 * LANES)
    packed = pl.cdiv(tm, ch) * ch
    row = pl.BlockSpec((tm, d), lambda i, e, c: (i, 0))
    vec = pl.BlockSpec((1, d), lambda i, e, c: (0, 0))
    return pl.pallas_call(
        functools.partial(_moe_ln_body, alpha=alpha, n_experts=n_experts, n_rows=m, ch=ch),
        grid=(pl.cdiv(m, tm), n_experts, f // fc),
        in_specs=[row,
                  pl.BlockSpec((LANES, d), lambda i, e, c: (0, 0)),
                  pl.BlockSpec((None, None, d, fc), lambda i, e, c: (layer, e, 0, c)),
                  pl.BlockSpec((None, None, d, fc), lambda i, e, c: (layer, e, 0, c)),
                  pl.BlockSpec((None, None, fc, d), lambda i, e, c: (layer, e, c, 0)),
                  vec, vec],
        out_specs=row,
        out_shape=jax.ShapeDtypeStruct((m, d), F32),
        scratch_shapes=[pltpu.VMEM((tm, d), BF16),
                        pltpu.VMEM((LANES, tm), F32),
                        pltpu.VMEM((tm, LANES), F32),
                        pltpu.VMEM((tm, LANES), F32),
                        pltpu.VMEM((packed, d), BF16),
                        pltpu.VMEM((packed, d), F32),
                        pltpu.VMEM((tm, d), F32)],
        compiler_params=_params(3),
        name="moe_ln",
    )(x, wrt, w1, w3, w2, g, b)


def kernel(x_prompt, x_sample, cache_k, cache_v, page_table, w_qkv, w_o, ln1_g, ln1_b,
           ln2_g, ln2_b, ffn_w1, ffn_w3, ffn_w2, moe_router, moe_w1, moe_w3, moe_w2):
    n_batch, seq, d = x_prompt.shape
    n_seq = x_sample.shape[0]
    depth = w_qkv.shape[0]
    n_heads, hd = cache_k.shape[3], cache_k.shape[4]
    assert n_heads == N_HEADS and n_heads * hd == d and x_sample.shape[1] == 1
    assert seq % MOBA_BLOCK == 0 and n_heads % HEAD_GROUP == 0
    n_experts = moe_router.shape[2]
    alpha = (2.0 * depth) ** 0.25
    q_scale = hd ** -0.5
    slopes = jnp.exp2(-8.0 * jnp.arange(1, n_heads + 1, dtype=F32) / n_heads)
    cache_kt = jnp.transpose(cache_k, (0, 1, 3, 4, 2))
    cache_vt = jnp.transpose(cache_v, (0, 1, 3, 4, 2))
    mw1, mw3, mw2 = moe_w1.astype(BF16), moe_w3.astype(BF16), moe_w2.astype(BF16)

    xp = x_prompt.reshape(n_batch * seq, d)
    xs = x_sample.reshape(n_seq, d)
    new_kp, new_vp, new_ks, new_vs = [], [], [], []
    for i in range(depth):
        kind = "moba" if i % 2 == 0 else "sb"
        wq, wk, wv = w_qkv[i, :, :d], w_qkv[i, :, d:2 * d], w_qkv[i, :, 2 * d:]
        q_scale_prompt = q_scale * LOG2E if kind == "moba" else q_scale
        kp, vp, kb, qt, vt, km = _qkv_prompt(
            xp, wk.astype(BF16), wq.T.astype(BF16), wk.T.astype(BF16), wv.T.astype(BF16),
            q_scale_prompt, n_batch)
        op = _prompt_attention(kind, qt, kb, vt, km, slopes, n_batch)

        qkv_s = _linear(xs, w_qkv[i].astype(BF16), d)
        qs = (qkv_s[:, :d] * q_scale).reshape(n_seq, n_heads, hd)
        ks_ = qkv_s[:, d:2 * d].reshape(n_seq, n_heads, hd)
        vs = qkv_s[:, 2 * d:].reshape(n_seq, n_heads, hd)
        os_ = _decode_attention(kind, i, qs, ks_, vs, cache_kt, cache_vt, page_table, slopes)

        wo = w_o[i].astype(BF16)
        g1, b1 = ln1_g[i].reshape(1, d), ln1_b[i].reshape(1, d)
        g2, b2 = ln2_g[i].reshape(1, d), ln2_b[i].reshape(1, d)
        xp = _wo_ln(op, xp, wo, g1, b1, alpha, 512)
        xs = _wo_ln(os_.reshape(n_seq, d), xs, wo, g1, b1, alpha, n_seq)
        j = i // 2
        if i % 2 == 0:
            w1, w3, w2 = ffn_w1[j].astype(BF16), ffn_w3[j].astype(BF16), ffn_w2[j].astype(BF16)
            xp = _ffn_ln(xp, w1, w3, w2, g2, b2, alpha, 256)
            xs = _ffn_ln(xs, w1, w3, w2, g2, b2, alpha, n_seq)
        else:
            wrt = jnp.zeros((LANES, d), F32).at[:n_experts].set(moe_router[j].T)
            xp = _moe_ln(xp, wrt, mw1, mw3, mw2, j, g2, b2, alpha, MOE_TILE, MOE_FF_CHUNK)
            xs = _moe_ln(xs, wrt, mw1, mw3, mw2, j, g2, b2, alpha, n_seq, MOE_FF_CHUNK)
        new_kp.append(kp)
        new_vp.append(vp)
        new_ks.append(ks_)
        new_vs.append(vs)

    def stack_prompt(parts):
        stacked = jnp.stack(parts).reshape(depth, n_batch, n_heads, hd, seq)
        return jnp.transpose(stacked, (0, 1, 4, 2, 3))

    def stack_sample(parts):
        return jnp.stack(parts).reshape(depth, n_seq, 1, n_heads, hd)

    return (xp.reshape(n_batch, seq, d), xs.reshape(n_seq, 1, d),
            stack_prompt(new_kp), stack_prompt(new_vp), stack_sample(new_ks), stack_sample(new_vs))
```

```python
import functools
import math

import jax
import jax.numpy as jnp
from jax import lax
from jax.experimental import pallas as pl
from jax.experimental.pallas import tpu as pltpu

N_HEADS = 16
MOBA_BLOCK = 256
MOBA_TOPK = 3
EXPERT_TOPK = 2
LN_EPS = 1e-5
NEG = -1e30
LANES = 128
SUBLANES = 8
VMEM_LIMIT = 56 * 1024 * 1024
HEAD_GROUP = 8
PAGES_PER_STEP = 8
MOE_TILE = 896
MOE_FF_CHUNK = 896
LOG2E = math.log2(math.e)
SB_DEAD = -104.0
SB_FIRST_PAGES = 2

F32 = jnp.float32
BF16 = jnp.bfloat16
_NT = (((1,), (1,)), ((), ()))
_HIGHEST = lax.Precision.HIGHEST


def _params(n_grid_dims):
    return pltpu.CompilerParams(
        dimension_semantics=("arbitrary",) * n_grid_dims,
        vmem_limit_bytes=VMEM_LIMIT)


def _resident(block_shape, index_map):
    return pl.BlockSpec(block_shape, index_map, pipeline_mode=pl.Buffered(1))


def _layer_norm(y, g, b):
    mu = jnp.mean(y, axis=-1, keepdims=True)
    yc = y - mu
    var = jnp.mean(yc * yc, axis=-1, keepdims=True)
    return yc * lax.rsqrt(var + LN_EPS) * g + b


def _silu(x):
    return x * jax.nn.sigmoid(x)


def _log_sigmoid_pair(z):
    sp = jnp.log(1.0 + jnp.exp(-jnp.abs(z)))
    ls = jnp.minimum(z, 0.0) - sp
    return ls, ls - z


def _split_bf16(x):
    hi = x.astype(BF16)
    lo = (x - hi.astype(F32)).astype(BF16)
    return hi, lo


def _qkv_prompt_body(x_ref, wk_ref, wqt_ref, wkt_ref, wvt_ref, *refs, q_scale):
    kt_ref, vt_ref, kb_ref, qt_ref, vtb_ref, km_ref = refs[-6:]
    xb = x_ref[...].astype(BF16)
    k = jnp.dot(xb, wk_ref[...], preferred_element_type=F32)
    kb_ref[...] = k.astype(BF16)
    km_ref[0] = jnp.mean(k, axis=0, keepdims=True)
    qt = lax.dot_general(wqt_ref[...], xb, _NT, preferred_element_type=F32)
    qt_ref[0] = qt * q_scale
    kt_ref[0] = lax.dot_general(wkt_ref[...], xb, _NT, preferred_element_type=F32)
    vt = lax.dot_general(wvt_ref[...], xb, _NT, preferred_element_type=F32)
    vt_ref[0] = vt
    vtb_ref[0] = vt.astype(BF16)


def _qkv_prompt(x, wk, wqt, wkt, wvt, q_scale, n_batch, layer, depth, stacks):
    m, d = x.shape
    blk = MOBA_BLOCK
    nt = m // blk
    nq = nt // n_batch
    w_spec = _resident((d, d), lambda i: (0, 0))
    seq_major = pl.BlockSpec((None, 1, d, blk), lambda i: (layer, i // nq, 0, i % nq))
    in_specs = [pl.BlockSpec((blk, d), lambda i: (i, 0)), w_spec, w_spec, w_spec, w_spec]
    operands = [x, wk, wqt, wkt, wvt]
    aliases = {}
    if stacks is not None:
        aliases = {len(operands): 0, len(operands) + 1: 1}
        in_specs += [pl.BlockSpec(memory_space=pl.ANY)] * 2
        operands += list(stacks)
    return pl.pallas_call(
        functools.partial(_qkv_prompt_body, q_scale=q_scale),
        grid=(nt,),
        in_specs=in_specs,
        input_output_aliases=aliases,
        out_specs=[
            seq_major,
            seq_major,
            pl.BlockSpec((blk, d), lambda i: (i, 0)),
            pl.BlockSpec((1, d, blk), lambda i: (i, 0, 0)),
            pl.BlockSpec((1, d, blk), lambda i: (i, 0, 0)),
            pl.BlockSpec((1, 1, d), lambda i: (i, 0, 0)),
        ],
        out_shape=[
            jax.ShapeDtypeStruct((depth, n_batch, d, nq * blk), F32),
            jax.ShapeDtypeStruct((depth, n_batch, d, nq * blk), F32),
            jax.ShapeDtypeStruct((m, d), BF16),
            jax.ShapeDtypeStruct((nt, d, blk), F32),
            jax.ShapeDtypeStruct((nt, d, blk), BF16),
            jax.ShapeDtypeStruct((nt, 1, d), F32),
        ],
        compiler_params=_params(1),
        name="qkv_prompt",
    )(*operands)


def _linear_body(x_ref, w_ref, y_ref):
    y_ref[...] = jnp.dot(x_ref[...].astype(BF16), w_ref[...], preferred_element_type=F32)


def _linear(x, w, tn):
    m, d = x.shape
    n = w.shape[1]
    return pl.pallas_call(
        _linear_body,
        grid=(n // tn,),
        in_specs=[pl.BlockSpec((m, d), lambda j: (0, 0)), pl.BlockSpec((d, tn), lambda j: (0, j))],
        out_specs=pl.BlockSpec((m, tn), lambda j: (0, j)),
        out_shape=jax.ShapeDtypeStruct((m, n), F32),
        compiler_params=_params(1),
        name="qkv_sample",
    )(x, w)


def _head_operands(qt_ref, hh, hd):
    pr, half = divmod(hh, 2)
    pair = 2 * hd
    qpair = qt_ref[0, pr * pair:(pr + 1) * pair, :]
    prow = lax.broadcasted_iota(jnp.int32, qpair.shape, 0)
    return jnp.where((prow // hd) == half, qpair, 0.0)


def _moba_prompt_body(slope_ref, qt_ref, kb_ref, vt_ref, km_ref, o_ref,
                      bias_ref, ab_ref, *, hd):
    hg = pl.program_id(1)
    i = pl.program_id(2)
    blk = MOBA_BLOCK
    pair = 2 * hd
    nb = km_ref.shape[1]
    row = lax.broadcasted_iota(jnp.int32, (blk, blk), 0)
    col = lax.broadcasted_iota(jnp.int32, (blk, blk), 1)
    bidx = lax.broadcasted_iota(jnp.int32, (nb, blk), 0)

    @pl.when(i == 0)
    def _():
        rowf = row.astype(F32)
        for hh in range(HEAD_GROUP):
            ab_ref[hh] = slope_ref[HEAD_GROUP * hg + hh] * rowf

    valid = bidx < i
    q2bs = []
    state = []
    for hh in range(HEAD_GROUP):
        pr = hh // 2
        q2 = _head_operands(qt_ref, hh, hd)
        q2b = q2.astype(BF16)
        q2bs.append(q2b)
        gate = jnp.dot(km_ref[0, :, pr * pair:(pr + 1) * pair], q2, precision=_HIGHEST,
                       preferred_element_type=F32)
        g = jnp.where(valid, gate, -jnp.inf)
        rank = jnp.zeros((nb, blk), jnp.int32)
        for m in range(nb):
            gm = g[m:m + 1, :]
            beats = (gm > g) | ((gm == g) & (m < bidx))
            rank = rank + jnp.where(beats, 1, 0)
        sel = valid & (rank < MOBA_TOPK)
        bias_ref[hh] = jnp.where(sel, 0.0, NEG)

        kd = kb_ref[pl.ds(pl.multiple_of(i * blk, blk), blk), pr * pair:(pr + 1) * pair]
        t = jnp.dot(kd, q2b, preferred_element_type=F32) + ab_ref[hh]
        t = jnp.where(row <= col, t, NEG)
        m0 = jnp.max(t, axis=0, keepdims=True)
        p = jnp.exp2(t - m0)
        state.append((m0, jnp.sum(p, axis=0, keepdims=True),
                      jnp.dot(vt_ref[0, i, hh * hd:(hh + 1) * hd, :], p.astype(BF16),
                              preferred_element_type=F32)))

    def body(j, carry):
        heads = range(HEAD_GROUP)
        kjs = [kb_ref[pl.ds(pl.multiple_of(j * blk, blk), blk), pr * pair:(pr + 1) * pair]
               for pr in range(HEAD_GROUP // 2)]
        ts = [jnp.dot(kjs[hh // 2], q2bs[hh], preferred_element_type=F32) + ab_ref[hh]
              for hh in heads]
        brows = [bias_ref[hh, pl.ds(j, 1), :]
                 + slope_ref[HEAD_GROUP * hg + hh] * ((j - i) * blk).astype(F32) for hh in heads]
        m_news = [jnp.maximum(carry[hh][0], jnp.max(ts[hh], axis=0, keepdims=True) + brows[hh])
                  for hh in heads]
        ps = [jnp.exp2(ts[hh] - (m_news[hh] - brows[hh])) for hh in heads]
        pvs = [jnp.dot(vt_ref[0, j, hh * hd:(hh + 1) * hd, :], ps[hh].astype(BF16),
                       preferred_element_type=F32) for hh in heads]
        out = []
        for hh in heads:
            m_run, l_run, acc = carry[hh]
            a = jnp.exp2(m_run - m_news[hh])
            out.append((m_news[hh], a * l_run + jnp.sum(ps[hh], axis=0, keepdims=True),
                        a * acc + pvs[hh]))
        return tuple(out)

    state = lax.fori_loop(0, i, body, tuple(state))
    outs = [acc / l_run for (_, l_run, acc) in state]
    o_ref[...] = jnp.concatenate(outs, axis=0).T.astype(o_ref.dtype)


def _sb_prompt_body(qt_ref, kb_ref, vt_ref, o_ref, *, hd):
    i = pl.program_id(2)
    blk = MOBA_BLOCK
    pair = 2 * hd
    row = lax.broadcasted_iota(jnp.int32, (blk, blk), 0)
    col = lax.broadcasted_iota(jnp.int32, (blk, blk), 1)
    upper = jnp.where(col > row, 1.0, 0.0).astype(BF16)
    causal = row < col
    q2bs = [_head_operands(qt_ref, hh, hd).astype(BF16) for hh in range(HEAD_GROUP)]

    def tile(j, state, diag):
        heads = range(HEAD_GROUP)
        kjs = [kb_ref[pl.ds(pl.multiple_of(j * blk, blk), blk), pr * pair:(pr + 1) * pair]
               for pr in range(HEAD_GROUP // 2)]
        zs = [jnp.dot(kjs[hh // 2], q2bs[hh], preferred_element_type=F32) for hh in heads]
        out = []
        for hh in heads:
            ls, lk = _log_sigmoid_pair(zs[hh])
            if diag:
                lk = jnp.where(causal, lk, 0.0)
            hi, lo = _split_bf16(lk)
            suffix = (jnp.dot(upper, hi, preferred_element_type=F32)
                      + jnp.dot(upper, lo, preferred_element_type=F32))
            e = ls + suffix
            if not diag:
                e = e + state[hh][0]
            a = jnp.exp(e)
            if diag:
                a = jnp.where(causal, a, 0.0)
            pv = jnp.dot(vt_ref[0, j, hh * hd:(hh + 1) * hd, :], a.astype(BF16),
                         preferred_element_type=F32)
            lk_sum = jnp.sum(lk, axis=0, keepdims=True)
            if diag:
                out.append((lk_sum, pv))
            else:
                out.append((state[hh][0] + lk_sum, state[hh][1] + pv))
        return tuple(out)

    def live(state):
        return functools.reduce(jnp.maximum, [jnp.max(r) for r, _ in state])

    state = tile(i, None, True)

    def cond(c):
        return (c[0] >= 0) & (c[1] > SB_DEAD)

    def body(c):
        new = tile(c[0], c[2], False)
        return c[0] - 1, live(new), new

    _, _, state = lax.while_loop(cond, body, (i - 1, live(state), state))
    o_ref[...] = jnp.concatenate([acc for _, acc in state], axis=0).T.astype(o_ref.dtype)


def _prompt_attention(kind, qt, kb, vt, km, slopes, n_batch):
    nt, d, blk = qt.shape
    m = nt * blk
    nq = nt // n_batch
    t = nq * blk
    hd = d // N_HEADS
    gw = HEAD_GROUP * hd
    qt_spec = pl.BlockSpec((1, gw, blk), lambda b, h, i: (b * nq + i, h, 0))
    kb_spec = pl.BlockSpec((t, gw), lambda b, h, i: (b, h))
    vt_spec = pl.BlockSpec((1, nq, gw, blk), lambda b, h, i: (b, 0, h, 0))
    out_spec = pl.BlockSpec((blk, gw), lambda b, h, i: (b * nq + i, h))
    vt4 = vt.reshape(n_batch, nq, d, blk)
    grid = (n_batch, N_HEADS // HEAD_GROUP, nq)
    out_shape = jax.ShapeDtypeStruct((m, d), BF16)
    if kind == "moba":
        km_spec = pl.BlockSpec((1, nq, gw), lambda b, h, i: (b, 0, h))
        return pl.pallas_call(
            functools.partial(_moba_prompt_body, hd=hd),
            grid=grid,
            in_specs=[pl.BlockSpec(memory_space=pltpu.SMEM), qt_spec, kb_spec, vt_spec, km_spec],
            out_specs=out_spec,
            out_shape=out_shape,
            scratch_shapes=[pltpu.VMEM((HEAD_GROUP, nq, blk), F32),
                            pltpu.VMEM((HEAD_GROUP, blk, blk), F32)],
            compiler_params=_params(3),
            name="moba_prompt",
        )(slopes * LOG2E, qt, kb, vt4, km.reshape(n_batch, nq, d))
    return pl.pallas_call(
        functools.partial(_sb_prompt_body, hd=hd),
        grid=grid,
        in_specs=[qt_spec, kb_spec, vt_spec],
        out_specs=out_spec,
        out_shape=out_shape,
        compiler_params=_params(3),
        name="sb_prompt",
    )(qt, kb, vt4)


def _fill_q_lanes(qt_ref, qb_ref):
    qt = qt_ref[0]
    n_heads = qt.shape[1]
    for h in range(n_heads):
        qb_ref[h] = jnp.broadcast_to(qt[:, h:h + 1], qb_ref.shape[1:])


def _page_scores(k_ref, qb_ref):
    rows = [jnp.sum(k_ref[h] * qb_ref[h], axis=0, keepdims=True) for h in range(k_ref.shape[0])]
    return jnp.concatenate(rows, axis=0)


def _lane_sums(acc):
    ones = jnp.ones((SUBLANES, acc.shape[1]), F32)
    return lax.dot_general(ones, acc, _NT, precision=_HIGHEST, preferred_element_type=F32)[0:1]


def _moba_decode_body(pt_ref, q_ref, qt_ref, kn_ref, vn_ref, slope_ref, *refs,
                      n_pages, page, pps):
    del pt_ref
    k_refs, v_refs = refs[:pps], refs[pps:2 * pps]
    o_ref, qb_ref, m_ref, l_ref, gs_ref, acc_ref = refs[2 * pps:]
    st = pl.program_id(1)
    n_heads = q_ref.shape[1]
    ppb = MOBA_BLOCK // page
    nblk = n_pages // ppb
    slope = slope_ref[...]
    lane = lax.broadcasted_iota(jnp.int32, (1, page), 1)

    @pl.when(st == 0)
    def _():
        _fill_q_lanes(qt_ref, qb_ref)

    for bb in range(pps // ppb):
        blk = st * (pps // ppb) + bb
        raws = [_page_scores(k_refs[bb * ppb + g], qb_ref) for g in range(ppb)]
        gs_ref[blk] = sum(jnp.sum(r, axis=1, keepdims=True) for r in raws)
        ss = []
        for g in range(ppb):
            tpos = ((st * pps + bb * ppb + g) * page + lane).astype(F32)
            ss.append(raws[g] + slope * tpos)
        m = functools.reduce(jnp.maximum, [jnp.max(s, axis=1, keepdims=True) for s in ss])
        ps = [jnp.exp(s - m) for s in ss]
        m_ref[blk] = m
        l_ref[blk] = sum(jnp.sum(p, axis=1, keepdims=True) for p in ps)
        for h in range(n_heads):
            acc_ref[blk, h] = sum(ps[g][h:h + 1, :] * v_refs[bb * ppb + g][h] for g in range(ppb))

    @pl.when(st == pl.num_programs(1) - 1)
    def _():
        gate = gs_ref[...]
        bidx = lax.broadcasted_iota(jnp.int32, (nblk, 1, 1), 0)
        rank = jnp.zeros(gate.shape, jnp.int32)
        for mm in range(nblk):
            gm = gate[mm:mm + 1]
            beats = (gm > gate) | ((gm == gate) & (mm < bidx))
            rank = rank + jnp.where(beats, 1, 0)
        sel = rank < MOBA_TOPK
        s_own = (jnp.sum(kn_ref[0] * q_ref[0], axis=-1, keepdims=True)
                 + slope * float(n_pages * page))
        m_blk = m_ref[...]
        m_all = jnp.maximum(s_own, jnp.max(jnp.where(sel, m_blk, NEG), axis=0))
        w = jnp.where(sel, jnp.exp(m_blk - m_all[None]), 0.0)
        w_own = jnp.exp(s_own - m_all)
        l_all = w_own + jnp.sum(w * l_ref[...], axis=0)
        rows = []
        for h in range(n_heads):
            mixed = sum(w[n, h:h + 1, :] * acc_ref[n, h] for n in range(nblk))
            rows.append(_lane_sums(mixed))
        o_past = jnp.concatenate(rows, axis=0)
        o_ref[0] = (w_own * vn_ref[0] + o_past) / l_all


def _sb_decode_body(pt_ref, qt_ref, r0_ref, *refs, page, pps):
    del pt_ref
    k_refs, v_refs = refs[:pps], refs[pps:2 * pps]
    o_ref, r_out_ref, qb_ref, r_ref, acc_ref = refs[2 * pps:]
    st = pl.program_id(1)
    n_heads = acc_ref.shape[0]

    @pl.when(st == 0)
    def _():
        _fill_q_lanes(qt_ref, qb_ref)
        r_ref[...] = r0_ref[0]
        acc_ref[...] = jnp.zeros(acc_ref.shape, F32)

    src = lax.broadcasted_iota(jnp.int32, (page, page), 0)
    dst = lax.broadcasted_iota(jnp.int32, (page, page), 1)
    later = jnp.where(src > dst, 1.0, 0.0).astype(BF16)
    for g in range(pps):
        @pl.when(jnp.max(r_ref[...]) > SB_DEAD)
        def _(g=g):
            z = _page_scores(k_refs[g], qb_ref)
            ls, lk = _log_sigmoid_pair(z)
            hi, lo = _split_bf16(lk)
            suffix = (jnp.dot(hi, later, preferred_element_type=F32)
                      + jnp.dot(lo, later, preferred_element_type=F32))
            r_run = r_ref[...]
            a = jnp.exp(ls + suffix + r_run)
            for h in range(n_heads):
                acc_ref[h] = acc_ref[h] + a[h:h + 1, :] * v_refs[g][h]
            r_ref[...] = r_run + jnp.sum(lk, axis=1, keepdims=True)

    @pl.when(st == pl.num_programs(1) - 1)
    def _():
        o_ref[0] = jnp.concatenate([_lane_sums(acc_ref[h]) for h in range(n_heads)], axis=0)
        r_out_ref[0] = r_ref[...]


def _sb_decode_run(layer, qt, r0, page_ids, pps, cache_kt, cache_vt):
    n_seq, hd, n_heads = qt.shape
    page = cache_kt.shape[4]
    qt_spec = pl.BlockSpec((1, hd, n_heads), lambda b, s, pt: (b, 0, 0))
    r_spec = pl.BlockSpec((1, n_heads, 1), lambda b, s, pt: (b, 0, 0))
    pages = [pl.BlockSpec((None, None, n_heads, hd, page),
                          lambda b, s, pt, g=g: (layer, pt[b, s * pps + g], 0, 0, 0))
             for g in range(pps)]
    grid_spec = pltpu.PrefetchScalarGridSpec(
        num_scalar_prefetch=1,
        grid=(n_seq, page_ids.shape[1] // pps),
        in_specs=[qt_spec, r_spec] + pages + pages,
        out_specs=[pl.BlockSpec((1, n_heads, hd), lambda b, s, pt: (b, 0, 0)), r_spec],
        scratch_shapes=[pltpu.VMEM((n_heads, hd, page), F32), pltpu.VMEM((n_heads, 1), F32),
                        pltpu.VMEM((n_heads, hd, page), F32)])
    return pl.pallas_call(
        functools.partial(_sb_decode_body, page=page, pps=pps),
        grid_spec=grid_spec,
        out_shape=[jax.ShapeDtypeStruct((n_seq, n_heads, hd), F32),
                   jax.ShapeDtypeStruct((n_seq, n_heads, 1), F32)],
        compiler_params=_params(2),
        name="sb_decode",
    )(page_ids, qt, r0, *([cache_kt] * pps), *([cache_vt] * pps))


def _sb_decode(layer, q, cache_kt, cache_vt, page_table):
    n_seq, n_heads, _ = q.shape
    n_pages = page_table.shape[1]
    first = min(SB_FIRST_PAGES, n_pages)
    qt = jnp.swapaxes(q, 1, 2)
    newest_first = page_table[:, ::-1]
    o, r = _sb_decode_run(layer, qt, jnp.zeros((n_seq, n_heads, 1), F32),
                          newest_first[:, :first], first, cache_kt, cache_vt)
    rest = n_pages - first
    if rest:
        live = jnp.max(r, axis=(1, 2)) > SB_DEAD
        older = jnp.where(live[:, None], newest_first[:, first:], page_table[0, 0])
        pps = max(p for p in range(1, PAGES_PER_STEP + 1) if rest % p == 0)
        o_old, _ = _sb_decode_run(layer, qt, r, older, pps, cache_kt, cache_vt)
        o = o + o_old
    return o


def _decode_attention(kind, layer, q, k_new, v_new, cache_kt, cache_vt, page_table, slopes):
    n_seq, n_heads, hd = q.shape
    n_pages = page_table.shape[1]
    page = cache_kt.shape[4]
    pps = PAGES_PER_STEP
    assert n_pages % pps == 0
    n_steps = n_pages // pps
    qt = jnp.swapaxes(q, 1, 2)
    tok_spec = pl.BlockSpec((1, n_heads, hd), lambda b, s, pt: (b, 0, 0))
    qt_spec = pl.BlockSpec((1, hd, n_heads), lambda b, s, pt: (b, 0, 0))
    out_shape = jax.ShapeDtypeStruct((n_seq, n_heads, hd), F32)
    qb = pltpu.VMEM((n_heads, hd, page), F32)

    def page_spec(page_of_step):
        return pl.BlockSpec((None, None, n_heads, hd, page),
                            lambda b, s, pt: (layer, pt[b, page_of_step(s)], 0, 0, 0))

    if kind == "moba":
        assert MOBA_BLOCK % page == 0 and pps % (MOBA_BLOCK // page) == 0
        nblk = n_pages * page // MOBA_BLOCK
        pages = [page_spec(lambda s, g=g: s * pps + g) for g in range(pps)]
        stat = pltpu.VMEM((nblk, n_heads, 1), F32)
        grid_spec = pltpu.PrefetchScalarGridSpec(
            num_scalar_prefetch=1,
            grid=(n_seq, n_steps),
            in_specs=[tok_spec, qt_spec, tok_spec, tok_spec,
                      pl.BlockSpec((n_heads, 1), lambda b, s, pt: (0, 0))] + pages + pages,
            out_specs=tok_spec,
            scratch_shapes=[qb, stat, stat, stat, pltpu.VMEM((nblk, n_heads, hd, page), F32)])
        return pl.pallas_call(
            functools.partial(_moba_decode_body, n_pages=n_pages, page=page, pps=pps),
            grid_spec=grid_spec,
            out_shape=out_shape,
            compiler_params=_params(2),
            name="moba_decode",
        )(page_table, q, qt, k_new, v_new, slopes.reshape(n_heads, 1),
          *([cache_kt] * pps), *([cache_vt] * pps))
    return _sb_decode(layer, q, cache_kt, cache_vt, page_table)


def _wo_ln_body(o_ref, x_ref, w_ref, g_ref, b_ref, y_ref, *, alpha):
    y = alpha * x_ref[...] + jnp.dot(o_ref[...].astype(BF16), w_ref[...],
                                     preferred_element_type=F32)
    y_ref[...] = _layer_norm(y, g_ref[...], b_ref[...])


def _wo_ln(o, x, w, g, b, alpha, tm):
    m, d = x.shape
    row = pl.BlockSpec((tm, d), lambda i: (i, 0))
    vec = pl.BlockSpec((1, d), lambda i: (0, 0))
    return pl.pallas_call(
        functools.partial(_wo_ln_body, alpha=alpha),
        grid=(m // tm,),
        in_specs=[row, row, _resident((d, d), lambda i: (0, 0)), vec, vec],
        out_specs=row,
        out_shape=jax.ShapeDtypeStruct((m, d), F32),
        compiler_params=_params(1),
        name="wo_ln",
    )(o, x, w, g, b)


def _ffn_ln_body(x_ref, w1_ref, w3_ref, w2_ref, g_ref, b_ref, y_ref, *, alpha):
    x = x_ref[...]
    xb = x.astype(BF16)
    h1 = jnp.dot(xb, w1_ref[...], preferred_element_type=F32)
    h3 = jnp.dot(xb, w3_ref[...], preferred_element_type=F32)
    h = (_silu(h1) * h3).astype(BF16)
    y = alpha * x + jnp.dot(h, w2_ref[...], preferred_element_type=F32)
    y_ref[...] = _layer_norm(y, g_ref[...], b_ref[...])


def _ffn_ln(x, w1, w3, w2, g, b, alpha, tm):
    m, d = x.shape
    f = w1.shape[1]
    row = pl.BlockSpec((tm, d), lambda i: (i, 0))
    vec = pl.BlockSpec((1, d), lambda i: (0, 0))
    return pl.pallas_call(
        functools.partial(_ffn_ln_body, alpha=alpha),
        grid=(m // tm,),
        in_specs=[row, _resident((d, f), lambda i: (0, 0)), _resident((d, f), lambda i: (0, 0)),
                  _resident((f, d), lambda i: (0, 0)), vec, vec],
        out_specs=row,
        out_shape=jax.ShapeDtypeStruct((m, d), F32),
        compiler_params=_params(1),
        name="ffn_ln",
    )(x, w1, w3, w2, g, b)


def _moe_ln_body(x_ref, wrt_ref, w1_ref, w3_ref, w2_ref, g_ref, b_ref, y_ref,
                 xb_ref, rank_ref, combt_ref, rankt_ref, xg_ref, acc_ref, yacc_ref,
                 *, alpha, n_experts, n_rows, ch):
    i = pl.program_id(0)
    e = pl.program_id(1)
    c = pl.program_id(2)
    last_c = pl.num_programs(2) - 1
    tm = x_ref.shape[0]

    @pl.when((e == 0) & (c == 0))
    def _route():
        rows_left = n_rows - i * tm
        rid = lax.broadcasted_iota(jnp.int32, (tm, 1), 0)
        x = jnp.where(rid < rows_left, x_ref[...], 0.0)
        xb_ref[...] = x.astype(BF16)
        logits = lax.dot_general(wrt_ref[...], x, _NT, precision=_HIGHEST,
                                 preferred_element_type=F32)
        eidx = lax.broadcasted_iota(jnp.int32, logits.shape, 0)
        lg = jnp.where(eidx < n_experts, logits, -jnp.inf)
        m1 = jnp.max(lg, axis=0, keepdims=True)
        i1 = jnp.min(jnp.where(lg == m1, eidx, LANES), axis=0, keepdims=True)
        lg2 = jnp.where(eidx == i1, -jnp.inf, lg)
        m2 = jnp.max(lg2, axis=0, keepdims=True)
        i2 = jnp.min(jnp.where(lg2 == m2, eidx, LANES), axis=0, keepdims=True)
        t = jnp.exp(m2 - m1)
        g1 = 1.0 / (1.0 + t)
        g2 = t * g1
        comb = jnp.where(eidx == i1, g1, 0.0) + jnp.where(eidx == i2, g2, 0.0)
        tid = lax.broadcasted_iota(jnp.int32, (1, tm), 1)
        comb = jnp.where(tid < rows_left, comb, 0.0)
        routed = comb > 0.0
        src = lax.broadcasted_iota(jnp.int32, (tm, tm), 0)
        dst = lax.broadcasted_iota(jnp.int32, (tm, tm), 1)
        before = jnp.where(src < dst, 1.0, 0.0).astype(BF16)
        count = jnp.dot(jnp.where(routed, 1.0, 0.0).astype(BF16), before,
                        preferred_element_type=F32)
        rank = jnp.where(routed, count, -1.0)
        rank_ref[...] = rank
        combt_ref[...] = comb.T
        rankt_ref[...] = rank.T
        yacc_ref[...] = jnp.zeros(yacc_ref.shape, F32)

    rank_e = rank_ref[pl.ds(e, 1), :]
    n_e = jnp.sum(jnp.where(rank_e >= 0.0, 1, 0))
    n_chunks = (n_e + ch - 1) // ch

    def chunk_rows(r):
        return pl.ds(pl.multiple_of(r * ch, ch), ch)

    @pl.when(c == 0)
    def _gather():
        def body(r, carry):
            want = (lax.broadcasted_iota(jnp.int32, (ch, tm), 0) + r * ch).astype(F32)
            pick = jnp.where(rank_e == want, 1.0, 0.0).astype(BF16)
            xg_ref[chunk_rows(r), :] = jnp.dot(
                pick, xb_ref[...], preferred_element_type=F32).astype(BF16)
            return carry
        lax.fori_loop(0, n_chunks, body, 0)

    def expert_chunk(r, first):
        xg = xg_ref[chunk_rows(r), :]
        h1 = jnp.dot(xg, w1_ref[...], preferred_element_type=F32)
        h3 = jnp.dot(xg, w3_ref[...], preferred_element_type=F32)
        part = jnp.dot((_silu(h1) * h3).astype(BF16), w2_ref[...], preferred_element_type=F32)
        if first:
            acc_ref[chunk_rows(r), :] = part
        else:
            acc_ref[chunk_rows(r), :] += part

    @pl.when(c == 0)
    def _first_chunk():
        def body(r, carry):
            expert_chunk(r, True)
            return carry
        lax.fori_loop(0, n_chunks, body, 0)

    @pl.when(c > 0)
    def _later_chunk():
        def body(r, carry):
            expert_chunk(r, False)
            return carry
        lax.fori_loop(0, n_chunks, body, 0)

    @pl.when(c == last_c)
    def _scatter():
        lane = lax.broadcasted_iota(jnp.int32, (tm, LANES), 1)
        rank_col = jnp.sum(jnp.where(lane == e, rankt_ref[...], 0.0), axis=1, keepdims=True)
        gate_col = jnp.sum(jnp.where(lane == e, combt_ref[...], 0.0), axis=1, keepdims=True)

        def body(r, carry):
            want = (lax.broadcasted_iota(jnp.int32, (tm, ch), 1) + r * ch).astype(F32)
            place = jnp.where(rank_col == want, 1.0, 0.0).astype(BF16)
            yacc_ref[...] += gate_col * jnp.dot(
                place, acc_ref[chunk_rows(r), :].astype(BF16), preferred_element_type=F32)
            return carry
        lax.fori_loop(0, n_chunks, body, 0)

    @pl.when((e == n_experts - 1) & (c == last_c))
    def _finish():
        y = alpha * x_ref[...] + yacc_ref[...]
        y_ref[...] = _layer_norm(y, g_ref[...], b_ref[...])


def _moe_ln(x, wrt, w1, w3, w2, layer, g, b, alpha, tm, fc):
    m, d = x.shape
    n_experts, f = w1.shape[1], w1.shape[3]
    ch = min(tm, 2 * LANES)
    packed = pl.cdiv(tm, ch) * ch
    row = pl.BlockSpec((tm, d), lambda i, e, c: (i, 0))
    vec = pl.BlockSpec((1, d), lambda i, e, c: (0, 0))
    return pl.pallas_call(
        functools.partial(_moe_ln_body, alpha=alpha, n_experts=n_experts, n_rows=m, ch=ch),
        grid=(pl.cdiv(m, tm), n_experts, f // fc),
        in_specs=[row,
                  pl.BlockSpec((LANES, d), lambda i, e, c: (0, 0)),
                  pl.BlockSpec((None, None, d, fc), lambda i, e, c: (layer, e, 0, c)),
                  pl.BlockSpec((None, None, d, fc), lambda i, e, c: (layer, e, 0, c)),
                  pl.BlockSpec((None, None, fc, d), lambda i, e, c: (layer, e, c, 0)),
                  vec, vec],
        out_specs=row,
        out_shape=jax.ShapeDtypeStruct((m, d), F32),
        scratch_shapes=[pltpu.VMEM((tm, d), BF16),
                        pltpu.VMEM((LANES, tm), F32),
                        pltpu.VMEM((tm, LANES), F32),
                        pltpu.VMEM((tm, LANES), F32),
                        pltpu.VMEM((packed, d), BF16),
                        pltpu.VMEM((packed, d), F32),
                        pltpu.VMEM((tm, d), F32)],
        compiler_params=_params(3),
        name="moe_ln",
    )(x, wrt, w1, w3, w2, g, b)


def kernel(x_prompt, x_sample, cache_k, cache_v, page_table, w_qkv, w_o, ln1_g, ln1_b,
           ln2_g, ln2_b, ffn_w1, ffn_w3, ffn_w2, moe_router, moe_w1, moe_w3, moe_w2):
    n_batch, seq, d = x_prompt.shape
    n_seq = x_sample.shape[0]
    depth = w_qkv.shape[0]
    n_heads, hd = cache_k.shape[3], cache_k.shape[4]
    assert n_heads == N_HEADS and n_heads * hd == d and x_sample.shape[1] == 1
    assert seq % MOBA_BLOCK == 0 and n_heads % HEAD_GROUP == 0
    n_experts = moe_router.shape[2]
    alpha = (2.0 * depth) ** 0.25
    q_scale = hd ** -0.5
    slopes = jnp.exp2(-8.0 * jnp.arange(1, n_heads + 1, dtype=F32) / n_heads)
    cache_kt = jnp.transpose(cache_k, (0, 1, 3, 4, 2))
    cache_vt = jnp.transpose(cache_v, (0, 1, 3, 4, 2))
    mw1, mw3, mw2 = moe_w1.astype(BF16), moe_w3.astype(BF16), moe_w2.astype(BF16)

    xp = x_prompt.reshape(n_batch * seq, d)
    xs = x_sample.reshape(n_seq, d)
    new_ks, new_vs = [], []
    stacks = None
    for i in range(depth):
        kind = "moba" if i % 2 == 0 else "sb"
        wq, wk, wv = w_qkv[i, :, :d], w_qkv[i, :, d:2 * d], w_qkv[i, :, 2 * d:]
        q_scale_prompt = q_scale * LOG2E if kind == "moba" else q_scale
        kp, vp, kb, qt, vt, km = _qkv_prompt(
            xp, wk.astype(BF16), wq.T.astype(BF16), wk.T.astype(BF16), wv.T.astype(BF16),
            q_scale_prompt, n_batch, i, depth, stacks)
        stacks = (kp, vp)
        op = _prompt_attention(kind, qt, kb, vt, km, slopes, n_batch)

        qkv_s = _linear(xs, w_qkv[i].astype(BF16), d)
        qs = (qkv_s[:, :d] * q_scale).reshape(n_seq, n_heads, hd)
        ks_ = qkv_s[:, d:2 * d].reshape(n_seq, n_heads, hd)
        vs = qkv_s[:, 2 * d:].reshape(n_seq, n_heads, hd)
        os_ = _decode_attention(kind, i, qs, ks_, vs, cache_kt, cache_vt, page_table, slopes)

        wo = w_o[i].astype(BF16)
        g1, b1 = ln1_g[i].reshape(1, d), ln1_b[i].reshape(1, d)
        g2, b2 = ln2_g[i].reshape(1, d), ln2_b[i].reshape(1, d)
        xp = _wo_ln(op, xp, wo, g1, b1, alpha, 512)
        xs = _wo_ln(os_.reshape(n_seq, d), xs, wo, g1, b1, alpha, n_seq)
        j = i // 2
        if i % 2 == 0:
            w1, w3, w2 = ffn_w1[j].astype(BF16), ffn_w3[j].astype(BF16), ffn_w2[j].astype(BF16)
            xp = _ffn_ln(xp, w1, w3, w2, g2, b2, alpha, 256)
            xs = _ffn_ln(xs, w1, w3, w2, g2, b2, alpha, n_seq)
        else:
            wrt = jnp.zeros((LANES, d), F32).at[:n_experts].set(moe_router[j].T)
            xp = _moe_ln(xp, wrt, mw1, mw3, mw2, j, g2, b2, alpha, MOE_TILE, MOE_FF_CHUNK)
            xs = _moe_ln(xs, wrt, mw1, mw3, mw2, j, g2, b2, alpha, n_seq, MOE_FF_CHUNK)
        new_ks.append(ks_)
        new_vs.append(vs)

    def token_major(stack):
        return jnp.transpose(stack.reshape(depth, n_batch, n_heads, hd, seq), (0, 1, 4, 2, 3))

    def stack_sample(parts):
        return jnp.stack(parts).reshape(depth, n_seq, 1, n_heads, hd)

    return (xp.reshape(n_batch, seq, d), xs.reshape(n_seq, 1, d),
            token_major(stacks[0]), token_major(stacks[1]),
            stack_sample(new_ks), stack_sample(new_vs))
```

```python
import functools
import math

import jax
import jax.numpy as jnp
from jax import lax
from jax.experimental import pallas as pl
from jax.experimental.pallas import tpu as pltpu

N_HEADS = 16
MOBA_BLOCK = 256
MOBA_TOPK = 3
EXPERT_TOPK = 2
LN_EPS = 1e-5
NEG = -1e30
LANES = 128
SUBLANES = 8
VMEM_LIMIT = 56 * 1024 * 1024
HEAD_GROUP = 8
PAGES_PER_STEP = 8
MOE_TILE = 896
MOE_FF_CHUNK = 896
LOG2E = math.log2(math.e)
SB_DEAD = -104.0
SB_FIRST_PAGES = 2

F32 = jnp.float32
BF16 = jnp.bfloat16
_NT = (((1,), (1,)), ((), ()))
_HIGHEST = lax.Precision.HIGHEST


def _params(n_grid_dims):
    return pltpu.CompilerParams(
        dimension_semantics=("arbitrary",) * n_grid_dims,
        vmem_limit_bytes=VMEM_LIMIT)


def _resident(block_shape, index_map):
    return pl.BlockSpec(block_shape, index_map, pipeline_mode=pl.Buffered(1))


def _layer_norm(y, g, b):
    mu = jnp.mean(y, axis=-1, keepdims=True)
    yc = y - mu
    var = jnp.mean(yc * yc, axis=-1, keepdims=True)
    return yc * lax.rsqrt(var + LN_EPS) * g + b


def _silu(x):
    return x * jax.nn.sigmoid(x)


def _log_sigmoid_pair(z):
    sp = jnp.log(1.0 + jnp.exp(-jnp.abs(z)))
    ls = jnp.minimum(z, 0.0) - sp
    return ls, ls - z


def _split_bf16(x):
    hi = x.astype(BF16)
    lo = (x - hi.astype(F32)).astype(BF16)
    return hi, lo


def _qkv_prompt_body(x_ref, wk_ref, wqt_ref, wkt_ref, wvt_ref, *refs, q_scale):
    kt_ref, vt_ref, kb_ref, qt_ref, vtb_ref, km_ref = refs[-6:]
    xb = x_ref[...].astype(BF16)
    k = jnp.dot(xb, wk_ref[...], preferred_element_type=F32)
    kb_ref[...] = k.astype(BF16)
    km_ref[0] = jnp.mean(k, axis=0, keepdims=True)
    qt = lax.dot_general(wqt_ref[...], xb, _NT, preferred_element_type=F32)
    qt_ref[0] = qt * q_scale
    kt_ref[0] = lax.dot_general(wkt_ref[...], xb, _NT, preferred_element_type=F32)
    vt = lax.dot_general(wvt_ref[...], xb, _NT, preferred_element_type=F32)
    vt_ref[0] = vt
    vtb_ref[0] = vt.astype(BF16)


def _qkv_prompt(x, wk, wqt, wkt, wvt, q_scale, n_batch, layer, depth, stacks):
    m, d = x.shape
    blk = MOBA_BLOCK
    nt = m // blk
    nq = nt // n_batch
    w_spec = _resident((d, d), lambda i: (0, 0))
    seq_major = pl.BlockSpec((None, 1, d, blk), lambda i: (layer, i // nq, 0, i % nq))
    in_specs = [pl.BlockSpec((blk, d), lambda i: (i, 0)), w_spec, w_spec, w_spec, w_spec]
    operands = [x, wk, wqt, wkt, wvt]
    aliases = {}
    if stacks is not None:
        aliases = {len(operands): 0, len(operands) + 1: 1}
        in_specs += [pl.BlockSpec(memory_space=pl.ANY)] * 2
        operands += list(stacks)
    return pl.pallas_call(
        functools.partial(_qkv_prompt_body, q_scale=q_scale),
        grid=(nt,),
        in_specs=in_specs,
        input_output_aliases=aliases,
        out_specs=[
            seq_major,
            seq_major,
            pl.BlockSpec((blk, d), lambda i: (i, 0)),
            pl.BlockSpec((1, d, blk), lambda i: (i, 0, 0)),
            pl.BlockSpec((1, d, blk), lambda i: (i, 0, 0)),
            pl.BlockSpec((1, 1, d), lambda i: (i, 0, 0)),
        ],
        out_shape=[
            jax.ShapeDtypeStruct((depth, n_batch, d, nq * blk), F32),
            jax.ShapeDtypeStruct((depth, n_batch, d, nq * blk), F32),
            jax.ShapeDtypeStruct((m, d), BF16),
            jax.ShapeDtypeStruct((nt, d, blk), F32),
            jax.ShapeDtypeStruct((nt, d, blk), BF16),
            jax.ShapeDtypeStruct((nt, 1, d), F32),
        ],
        compiler_params=_params(1),
        name="qkv_prompt",
    )(*operands)


def _linear_body(x_ref, w_ref, y_ref):
    y_ref[...] = jnp.dot(x_ref[...].astype(BF16), w_ref[...], preferred_element_type=F32)


def _linear(x, w, tn):
    m, d = x.shape
    n = w.shape[1]
    return pl.pallas_call(
        _linear_body,
        grid=(n // tn,),
        in_specs=[pl.BlockSpec((m, d), lambda j: (0, 0)), pl.BlockSpec((d, tn), lambda j: (0, j))],
        out_specs=pl.BlockSpec((m, tn), lambda j: (0, j)),
        out_shape=jax.ShapeDtypeStruct((m, n), F32),
        compiler_params=_params(1),
        name="qkv_sample",
    )(x, w)


def _head_operands(qt_ref, hh, hd):
    pr, half = divmod(hh, 2)
    pair = 2 * hd
    qpair = qt_ref[0, pr * pair:(pr + 1) * pair, :]
    prow = lax.broadcasted_iota(jnp.int32, qpair.shape, 0)
    return jnp.where((prow // hd) == half, qpair, 0.0)


def _moba_prompt_body(slope_ref, qt_ref, kb_ref, vt_ref, km_ref, o_ref,
                      bias_ref, ab_ref, *, hd):
    hg = pl.program_id(1)
    i = pl.program_id(2)
    blk = MOBA_BLOCK
    pair = 2 * hd
    nb = km_ref.shape[1]
    row = lax.broadcasted_iota(jnp.int32, (blk, blk), 0)
    col = lax.broadcasted_iota(jnp.int32, (blk, blk), 1)
    bidx = lax.broadcasted_iota(jnp.int32, (nb, blk), 0)

    @pl.when(i == 0)
    def _():
        rowf = row.astype(F32)
        for hh in range(HEAD_GROUP):
            ab_ref[hh] = slope_ref[HEAD_GROUP * hg + hh] * rowf

    valid = bidx < i
    q2bs = []
    state = []
    for hh in range(HEAD_GROUP):
        pr = hh // 2
        q2 = _head_operands(qt_ref, hh, hd)
        q2b = q2.astype(BF16)
        q2bs.append(q2b)
        gate = jnp.dot(km_ref[0, :, pr * pair:(pr + 1) * pair], q2, precision=_HIGHEST,
                       preferred_element_type=F32)
        g = jnp.where(valid, gate, -jnp.inf)
        rank = jnp.zeros((nb, blk), jnp.int32)
        for m in range(nb):
            gm = g[m:m + 1, :]
            beats = (gm > g) | ((gm == g) & (m < bidx))
            rank = rank + jnp.where(beats, 1, 0)
        sel = valid & (rank < MOBA_TOPK)
        bias_ref[hh] = jnp.where(sel, 0.0, NEG)

        kd = kb_ref[pl.ds(pl.multiple_of(i * blk, blk), blk), pr * pair:(pr + 1) * pair]
        t = jnp.dot(kd, q2b, preferred_element_type=F32) + ab_ref[hh]
        t = jnp.where(row <= col, t, NEG)
        m0 = jnp.max(t, axis=0, keepdims=True)
        p = jnp.exp2(t - m0)
        state.append((m0, jnp.sum(p, axis=0, keepdims=True),
                      jnp.dot(vt_ref[0, i, hh * hd:(hh + 1) * hd, :], p.astype(BF16),
                              preferred_element_type=F32)))

    def body(j, carry):
        heads = range(HEAD_GROUP)
        kjs = [kb_ref[pl.ds(pl.multiple_of(j * blk, blk), blk), pr * pair:(pr + 1) * pair]
               for pr in range(HEAD_GROUP // 2)]
        ts = [jnp.dot(kjs[hh // 2], q2bs[hh], preferred_element_type=F32) + ab_ref[hh]
              for hh in heads]
        brows = [bias_ref[hh, pl.ds(j, 1), :]
                 + slope_ref[HEAD_GROUP * hg + hh] * ((j - i) * blk).astype(F32) for hh in heads]
        m_news = [jnp.maximum(carry[hh][0], jnp.max(ts[hh], axis=0, keepdims=True) + brows[hh])
                  for hh in heads]
        ps = [jnp.exp2(ts[hh] - (m_news[hh] - brows[hh])) for hh in heads]
        pvs = [jnp.dot(vt_ref[0, j, hh * hd:(hh + 1) * hd, :], ps[hh].astype(BF16),
                       preferred_element_type=F32) for hh in heads]
        out = []
        for hh in heads:
            m_run, l_run, acc = carry[hh]
            a = jnp.exp2(m_run - m_news[hh])
            out.append((m_news[hh], a * l_run + jnp.sum(ps[hh], axis=0, keepdims=True),
                        a * acc + pvs[hh]))
        return tuple(out)

    state = lax.fori_loop(0, i, body, tuple(state))
    outs = [acc / l_run for (_, l_run, acc) in state]
    o_ref[...] = jnp.concatenate(outs, axis=0).T.astype(o_ref.dtype)


def _sb_prompt_body(qt_ref, kb_ref, vt_ref, o_ref, *, hd):
    i = pl.program_id(2)
    blk = MOBA_BLOCK
    pair = 2 * hd
    row = lax.broadcasted_iota(jnp.int32, (blk, blk), 0)
    col = lax.broadcasted_iota(jnp.int32, (blk, blk), 1)
    upper = jnp.where(col > row, 1.0, 0.0).astype(BF16)
    causal = row < col
    q2bs = [_head_operands(qt_ref, hh, hd).astype(BF16) for hh in range(HEAD_GROUP)]

    def tile(j, state, diag):
        heads = range(HEAD_GROUP)
        kjs = [kb_ref[pl.ds(pl.multiple_of(j * blk, blk), blk), pr * pair:(pr + 1) * pair]
               for pr in range(HEAD_GROUP // 2)]
        zs = [jnp.dot(kjs[hh // 2], q2bs[hh], preferred_element_type=F32) for hh in heads]
        out = []
        for hh in heads:
            ls, lk = _log_sigmoid_pair(zs[hh])
            if diag:
                lk = jnp.where(causal, lk, 0.0)
            hi, lo = _split_bf16(lk)
            suffix = (jnp.dot(upper, hi, preferred_element_type=F32)
                      + jnp.dot(upper, lo, preferred_element_type=F32))
            e = ls + suffix
            if not diag:
                e = e + state[hh][0]
            a = jnp.exp(e)
            if diag:
                a = jnp.where(causal, a, 0.0)
            pv = jnp.dot(vt_ref[0, j, hh * hd:(hh + 1) * hd, :], a.astype(BF16),
                         preferred_element_type=F32)
            lk_sum = jnp.sum(lk, axis=0, keepdims=True)
            if diag:
                out.append((lk_sum, pv))
            else:
                out.append((state[hh][0] + lk_sum, state[hh][1] + pv))
        return tuple(out)

    def live(state):
        return functools.reduce(jnp.maximum, [jnp.max(r) for r, _ in state])

    state = tile(i, None, True)

    def cond(c):
        return (c[0] >= 0) & (c[1] > SB_DEAD)

    def body(c):
        new = tile(c[0], c[2], False)
        return c[0] - 1, live(new), new

    _, _, state = lax.while_loop(cond, body, (i - 1, live(state), state))
    o_ref[...] = jnp.concatenate([acc for _, acc in state], axis=0).T.astype(o_ref.dtype)


def _prompt_attention(kind, qt, kb, vt, km, slopes, n_batch):
    nt, d, blk = qt.shape
    m = nt * blk
    nq = nt // n_batch
    t = nq * blk
    hd = d // N_HEADS
    gw = HEAD_GROUP * hd
    qt_spec = pl.BlockSpec((1, gw, blk), lambda b, h, i: (b * nq + i, h, 0))
    kb_spec = pl.BlockSpec((t, gw), lambda b, h, i: (b, h))
    vt_spec = pl.BlockSpec((1, nq, gw, blk), lambda b, h, i: (b, 0, h, 0))
    out_spec = pl.BlockSpec((blk, gw), lambda b, h, i: (b * nq + i, h))
    vt4 = vt.reshape(n_batch, nq, d, blk)
    grid = (n_batch, N_HEADS // HEAD_GROUP, nq)
    out_shape = jax.ShapeDtypeStruct((m, d), BF16)
    if kind == "moba":
        km_spec = pl.BlockSpec((1, nq, gw), lambda b, h, i: (b, 0, h))
        return pl.pallas_call(
            functools.partial(_moba_prompt_body, hd=hd),
            grid=grid,
            in_specs=[pl.BlockSpec(memory_space=pltpu.SMEM), qt_spec, kb_spec, vt_spec, km_spec],
            out_specs=out_spec,
            out_shape=out_shape,
            scratch_shapes=[pltpu.VMEM((HEAD_GROUP, nq, blk), F32),
                            pltpu.VMEM((HEAD_GROUP, blk, blk), F32)],
            compiler_params=_params(3),
            name="moba_prompt",
        )(slopes * LOG2E, qt, kb, vt4, km.reshape(n_batch, nq, d))
    return pl.pallas_call(
        functools.partial(_sb_prompt_body, hd=hd),
        grid=grid,
        in_specs=[qt_spec, kb_spec, vt_spec],
        out_specs=out_spec,
        out_shape=out_shape,
        compiler_params=_params(3),
        name="sb_prompt",
    )(qt, kb, vt4)


def _fill_q_lanes(qt_ref, qb_ref):
    qt = qt_ref[0]
    n_heads = qt.shape[1]
    for h in range(n_heads):
        qb_ref[h] = jnp.broadcast_to(qt[:, h:h + 1], qb_ref.shape[1:])


def _page_scores(k_ref, qb_ref):
    rows = [jnp.sum(k_ref[h] * qb_ref[h], axis=0, keepdims=True) for h in range(k_ref.shape[0])]
    return jnp.concatenate(rows, axis=0)


def _lane_sums(acc):
    ones = jnp.ones((SUBLANES, acc.shape[1]), F32)
    return lax.dot_general(ones, acc, _NT, precision=_HIGHEST, preferred_element_type=F32)[0:1]


def _moba_decode_body(pt_ref, q_ref, qt_ref, kn_ref, vn_ref, slope_ref, *refs,
                      n_pages, page, pps):
    del pt_ref
    k_refs, v_refs = refs[:pps], refs[pps:2 * pps]
    o_ref, qb_ref, m_ref, l_ref, gs_ref, acc_ref = refs[2 * pps:]
    st = pl.program_id(1)
    n_heads = q_ref.shape[1]
    ppb = MOBA_BLOCK // page
    nblk = n_pages // ppb
    slope = slope_ref[...]
    lane = lax.broadcasted_iota(jnp.int32, (1, page), 1)

    @pl.when(st == 0)
    def _():
        _fill_q_lanes(qt_ref, qb_ref)

    for bb in range(pps // ppb):
        blk = st * (pps // ppb) + bb
        raws = [_page_scores(k_refs[bb * ppb + g], qb_ref) for g in range(ppb)]
        gs_ref[blk] = sum(jnp.sum(r, axis=1, keepdims=True) for r in raws)
        ss = []
        for g in range(ppb):
            tpos = ((st * pps + bb * ppb + g) * page + lane).astype(F32)
            ss.append(raws[g] + slope * tpos)
        m = functools.reduce(jnp.maximum, [jnp.max(s, axis=1, keepdims=True) for s in ss])
        ps = [jnp.exp(s - m) for s in ss]
        m_ref[blk] = m
        l_ref[blk] = sum(jnp.sum(p, axis=1, keepdims=True) for p in ps)
        for h in range(n_heads):
            acc_ref[blk, h] = sum(ps[g][h:h + 1, :] * v_refs[bb * ppb + g][h] for g in range(ppb))

    @pl.when(st == pl.num_programs(1) - 1)
    def _():
        gate = gs_ref[...]
        bidx = lax.broadcasted_iota(jnp.int32, (nblk, 1, 1), 0)
        rank = jnp.zeros(gate.shape, jnp.int32)
        for mm in range(nblk):
            gm = gate[mm:mm + 1]
            beats = (gm > gate) | ((gm == gate) & (mm < bidx))
            rank = rank + jnp.where(beats, 1, 0)
        sel = rank < MOBA_TOPK
        s_own = (jnp.sum(kn_ref[0] * q_ref[0], axis=-1, keepdims=True)
                 + slope * float(n_pages * page))
        m_blk = m_ref[...]
        m_all = jnp.maximum(s_own, jnp.max(jnp.where(sel, m_blk, NEG), axis=0))
        w = jnp.where(sel, jnp.exp(m_blk - m_all[None]), 0.0)
        w_own = jnp.exp(s_own - m_all)
        l_all = w_own + jnp.sum(w * l_ref[...], axis=0)
        rows = []
        for h in range(n_heads):
            mixed = sum(w[n, h:h + 1, :] * acc_ref[n, h] for n in range(nblk))
            rows.append(_lane_sums(mixed))
        o_past = jnp.concatenate(rows, axis=0)
        o_ref[0] = (w_own * vn_ref[0] + o_past) / l_all


def _sb_decode_body(pt_ref, qt_ref, r0_ref, *refs, page, pps):
    del pt_ref
    k_refs, v_refs = refs[:pps], refs[pps:2 * pps]
    o_ref, r_out_ref, qb_ref, r_ref, acc_ref = refs[2 * pps:]
    st = pl.program_id(1)
    n_heads = acc_ref.shape[0]
    live0 = jnp.max(r0_ref[0]) > SB_DEAD

    @pl.when(st == 0)
    def _():
        r_ref[...] = r0_ref[0]

    @pl.when((st == 0) & live0)
    def _():
        _fill_q_lanes(qt_ref, qb_ref)
        acc_ref[...] = jnp.zeros(acc_ref.shape, F32)

    @pl.when(live0)
    def _pages():
        src = lax.broadcasted_iota(jnp.int32, (page, page), 0)
        dst = lax.broadcasted_iota(jnp.int32, (page, page), 1)
        later = jnp.where(src > dst, 1.0, 0.0).astype(BF16)
        for g in range(pps):
            @pl.when(jnp.max(r_ref[...]) > SB_DEAD)
            def _(g=g):
                z = _page_scores(k_refs[g], qb_ref)
                ls, lk = _log_sigmoid_pair(z)
                hi, lo = _split_bf16(lk)
                suffix = (jnp.dot(hi, later, preferred_element_type=F32)
                          + jnp.dot(lo, later, preferred_element_type=F32))
                r_run = r_ref[...]
                a = jnp.exp(ls + suffix + r_run)
                for h in range(n_heads):
                    acc_ref[h] = acc_ref[h] + a[h:h + 1, :] * v_refs[g][h]
                r_ref[...] = r_run + jnp.sum(lk, axis=1, keepdims=True)

    last = st == pl.num_programs(1) - 1

    @pl.when(last)
    def _():
        r_out_ref[0] = r_ref[...]

    @pl.when(last & live0)
    def _():
        o_ref[0] = jnp.concatenate([_lane_sums(acc_ref[h]) for h in range(n_heads)], axis=0)

    @pl.when(last & jnp.logical_not(live0))
    def _():
        o_ref[0] = jnp.zeros(o_ref.shape[1:], F32)


def _sb_decode_run(layer, qt, r0, page_ids, pps, cache_kt, cache_vt):
    n_seq, hd, n_heads = qt.shape
    page = cache_kt.shape[4]
    qt_spec = pl.BlockSpec((1, hd, n_heads), lambda b, s, pt: (b, 0, 0))
    r_spec = pl.BlockSpec((1, n_heads, 1), lambda b, s, pt: (b, 0, 0))
    pages = [pl.BlockSpec((None, None, n_heads, hd, page),
                          lambda b, s, pt, g=g: (layer, pt[b, s * pps + g], 0, 0, 0))
             for g in range(pps)]
    grid_spec = pltpu.PrefetchScalarGridSpec(
        num_scalar_prefetch=1,
        grid=(n_seq, page_ids.shape[1] // pps),
        in_specs=[qt_spec, r_spec] + pages + pages,
        out_specs=[pl.BlockSpec((1, n_heads, hd), lambda b, s, pt: (b, 0, 0)), r_spec],
        scratch_shapes=[pltpu.VMEM((n_heads, hd, page), F32), pltpu.VMEM((n_heads, 1), F32),
                        pltpu.VMEM((n_heads, hd, page), F32)])
    return pl.pallas_call(
        functools.partial(_sb_decode_body, page=page, pps=pps),
        grid_spec=grid_spec,
        out_shape=[jax.ShapeDtypeStruct((n_seq, n_heads, hd), F32),
                   jax.ShapeDtypeStruct((n_seq, n_heads, 1), F32)],
        compiler_params=_params(2),
        name="sb_decode",
    )(page_ids, qt, r0, *([cache_kt] * pps), *([cache_vt] * pps))


def _sb_decode(layer, q, cache_kt, cache_vt, page_table):
    n_seq, n_heads, _ = q.shape
    n_pages = page_table.shape[1]
    first = min(SB_FIRST_PAGES, n_pages)
    qt = jnp.swapaxes(q, 1, 2)
    newest_first = page_table[:, ::-1]
    o, r = _sb_decode_run(layer, qt, jnp.zeros((n_seq, n_heads, 1), F32),
                          newest_first[:, :first], first, cache_kt, cache_vt)
    rest = n_pages - first
    if rest:
        live = jnp.max(r, axis=(1, 2)) > SB_DEAD
        older = jnp.where(live[:, None], newest_first[:, first:], page_table[0, 0])
        pps = max(p for p in range(1, PAGES_PER_STEP + 1) if rest % p == 0)
        o_old, _ = _sb_decode_run(layer, qt, r, older, pps, cache_kt, cache_vt)
        o = o + o_old
    return o


def _decode_attention(kind, layer, q, k_new, v_new, cache_kt, cache_vt, page_table, slopes):
    n_seq, n_heads, hd = q.shape
    n_pages = page_table.shape[1]
    page = cache_kt.shape[4]
    pps = PAGES_PER_STEP
    assert n_pages % pps == 0
    n_steps = n_pages // pps
    qt = jnp.swapaxes(q, 1, 2)
    tok_spec = pl.BlockSpec((1, n_heads, hd), lambda b, s, pt: (b, 0, 0))
    qt_spec = pl.BlockSpec((1, hd, n_heads), lambda b, s, pt: (b, 0, 0))
    out_shape = jax.ShapeDtypeStruct((n_seq, n_heads, hd), F32)
    qb = pltpu.VMEM((n_heads, hd, page), F32)

    def page_spec(page_of_step):
        return pl.BlockSpec((None, None, n_heads, hd, page),
                            lambda b, s, pt: (layer, pt[b, page_of_step(s)], 0, 0, 0))

    if kind == "moba":
        assert MOBA_BLOCK % page == 0 and pps % (MOBA_BLOCK // page) == 0
        nblk = n_pages * page // MOBA_BLOCK
        pages = [page_spec(lambda s, g=g: s * pps + g) for g in range(pps)]
        stat = pltpu.VMEM((nblk, n_heads, 1), F32)
        grid_spec = pltpu.PrefetchScalarGridSpec(
            num_scalar_prefetch=1,
            grid=(n_seq, n_steps),
            in_specs=[tok_spec, qt_spec, tok_spec, tok_spec,
                      pl.BlockSpec((n_heads, 1), lambda b, s, pt: (0, 0))] + pages + pages,
            out_specs=tok_spec,
            scratch_shapes=[qb, stat, stat, stat, pltpu.VMEM((nblk, n_heads, hd, page), F32)])
        return pl.pallas_call(
            functools.partial(_moba_decode_body, n_pages=n_pages, page=page, pps=pps),
            grid_spec=grid_spec,
            out_shape=out_shape,
            compiler_params=_params(2),
            name="moba_decode",
        )(page_table, q, qt, k_new, v_new, slopes.reshape(n_heads, 1),
          *([cache_kt] * pps), *([cache_vt] * pps))
    return _sb_decode(layer, q, cache_kt, cache_vt, page_table)


def _wo_ln_body(o_ref, x_ref, w_ref, g_ref, b_ref, y_ref, *, alpha):
    y = alpha * x_ref[...] + jnp.dot(o_ref[...].astype(BF16), w_ref[...],
                                     preferred_element_type=F32)
    y_ref[...] = _layer_norm(y, g_ref[...], b_ref[...])


def _wo_ln(o, x, w, g, b, alpha, tm):
    m, d = x.shape
    row = pl.BlockSpec((tm, d), lambda i: (i, 0))
    vec = pl.BlockSpec((1, d), lambda i: (0, 0))
    return pl.pallas_call(
        functools.partial(_wo_ln_body, alpha=alpha),
        grid=(m // tm,),
        in_specs=[row, row, _resident((d, d), lambda i: (0, 0)), vec, vec],
        out_specs=row,
        out_shape=jax.ShapeDtypeStruct((m, d), F32),
        compiler_params=_params(1),
        name="wo_ln",
    )(o, x, w, g, b)


def _ffn_ln_body(x_ref, w1_ref, w3_ref, w2_ref, g_ref, b_ref, y_ref, *, alpha):
    x = x_ref[...]
    xb = x.astype(BF16)
    h1 = jnp.dot(xb, w1_ref[...], preferred_element_type=F32)
    h3 = jnp.dot(xb, w3_ref[...], preferred_element_type=F32)
    h = (_silu(h1) * h3).astype(BF16)
    y = alpha * x + jnp.dot(h, w2_ref[...], preferred_element_type=F32)
    y_ref[...] = _layer_norm(y, g_ref[...], b_ref[...])


def _ffn_ln(x, w1, w3, w2, g, b, alpha, tm):
    m, d = x.shape
    f = w1.shape[1]
    row = pl.BlockSpec((tm, d), lambda i: (i, 0))
    vec = pl.BlockSpec((1, d), lambda i: (0, 0))
    return pl.pallas_call(
        functools.partial(_ffn_ln_body, alpha=alpha),
        grid=(m // tm,),
        in_specs=[row, _resident((d, f), lambda i: (0, 0)), _resident((d, f), lambda i: (0, 0)),
                  _resident((f, d), lambda i: (0, 0)), vec, vec],
        out_specs=row,
        out_shape=jax.ShapeDtypeStruct((m, d), F32),
        compiler_params=_params(1),
        name="ffn_ln",
    )(x, w1, w3, w2, g, b)


def _moe_ln_body(x_ref, wrt_ref, w1_ref, w3_ref, w2_ref, g_ref, b_ref, y_ref,
                 xb_ref, rank_ref, combt_ref, rankt_ref, xg_ref, acc_ref, yacc_ref,
                 *, alpha, n_experts, n_rows, ch):
    i = pl.program_id(0)
    e = pl.program_id(1)
    c = pl.program_id(2)
    last_c = pl.num_programs(2) - 1
    tm = x_ref.shape[0]

    @pl.when((e == 0) & (c == 0))
    def _route():
        rows_left = n_rows - i * tm
        rid = lax.broadcasted_iota(jnp.int32, (tm, 1), 0)
        x = jnp.where(rid < rows_left, x_ref[...], 0.0)
        xb_ref[...] = x.astype(BF16)
        logits = lax.dot_general(wrt_ref[...], x, _NT, precision=_HIGHEST,
                                 preferred_element_type=F32)
        eidx = lax.broadcasted_iota(jnp.int32, logits.shape, 0)
        lg = jnp.where(eidx < n_experts, logits, -jnp.inf)
        m1 = jnp.max(lg, axis=0, keepdims=True)
        i1 = jnp.min(jnp.where(lg == m1, eidx, LANES), axis=0, keepdims=True)
        lg2 = jnp.where(eidx == i1, -jnp.inf, lg)
        m2 = jnp.max(lg2, axis=0, keepdims=True)
        i2 = jnp.min(jnp.where(lg2 == m2, eidx, LANES), axis=0, keepdims=True)
        t = jnp.exp(m2 - m1)
        g1 = 1.0 / (1.0 + t)
        g2 = t * g1
        comb = jnp.where(eidx == i1, g1, 0.0) + jnp.where(eidx == i2, g2, 0.0)
        tid = lax.broadcasted_iota(jnp.int32, (1, tm), 1)
        comb = jnp.where(tid < rows_left, comb, 0.0)
        routed = comb > 0.0
        src = lax.broadcasted_iota(jnp.int32, (tm, tm), 0)
        dst = lax.broadcasted_iota(jnp.int32, (tm, tm), 1)
        before = jnp.where(src < dst, 1.0, 0.0).astype(BF16)
        count = jnp.dot(jnp.where(routed, 1.0, 0.0).astype(BF16), before,
                        preferred_element_type=F32)
        rank = jnp.where(routed, count, -1.0)
        rank_ref[...] = rank
        combt_ref[...] = comb.T
        rankt_ref[...] = rank.T
        yacc_ref[...] = jnp.zeros(yacc_ref.shape, F32)

    rank_e = rank_ref[pl.ds(e, 1), :]
    n_e = jnp.sum(jnp.where(rank_e >= 0.0, 1, 0))
    n_chunks = (n_e + ch - 1) // ch

    def chunk_rows(r):
        return pl.ds(pl.multiple_of(r * ch, ch), ch)

    @pl.when(c == 0)
    def _gather():
        def body(r, carry):
            want = (lax.broadcasted_iota(jnp.int32, (ch, tm), 0) + r * ch).astype(F32)
            pick = jnp.where(rank_e == want, 1.0, 0.0).astype(BF16)
            xg_ref[chunk_rows(r), :] = jnp.dot(
                pick, xb_ref[...], preferred_element_type=F32).astype(BF16)
            return carry
        lax.fori_loop(0, n_chunks, body, 0)

    def expert_chunk(r, first):
        xg = xg_ref[chunk_rows(r), :]
        h1 = jnp.dot(xg, w1_ref[...], preferred_element_type=F32)
        h3 = jnp.dot(xg, w3_ref[...], preferred_element_type=F32)
        part = jnp.dot((_silu(h1) * h3).astype(BF16), w2_ref[...], preferred_element_type=F32)
        if first:
            acc_ref[chunk_rows(r), :] = part
        else:
            acc_ref[chunk_rows(r), :] += part

    @pl.when(c == 0)
    def _first_chunk():
        def body(r, carry):
            expert_chunk(r, True)
            return carry
        lax.fori_loop(0, n_chunks, body, 0)

    @pl.when(c > 0)
    def _later_chunk():
        def body(r, carry):
            expert_chunk(r, False)
            return carry
        lax.fori_loop(0, n_chunks, body, 0)

    @pl.when(c == last_c)
    def _scatter():
        lane = lax.broadcasted_iota(jnp.int32, (tm, LANES), 1)
        rank_col = jnp.sum(jnp.where(lane == e, rankt_ref[...], 0.0), axis=1, keepdims=True)
        gate_col = jnp.sum(jnp.where(lane == e, combt_ref[...], 0.0), axis=1, keepdims=True)

        def body(r, carry):
            want = (lax.broadcasted_iota(jnp.int32, (tm, ch), 1) + r * ch).astype(F32)
            place = jnp.where(rank_col == want, 1.0, 0.0).astype(BF16)
            yacc_ref[...] += gate_col * jnp.dot(
                place, acc_ref[chunk_rows(r), :].astype(BF16), preferred_element_type=F32)
            return carry
        lax.fori_loop(0, n_chunks, body, 0)

    @pl.when((e == n_experts - 1) & (c == last_c))
    def _finish():
        y = alpha * x_ref[...] + yacc_ref[...]
        y_ref[...] = _layer_norm(y, g_ref[...], b_ref[...])


def _moe_ln(x, wrt, w1, w3, w2, layer, g, b, alpha, tm, fc):
    m, d = x.shape
    n_experts, f = w1.shape[1], w1.shape[3]
    ch = min(tm, 2 * LANES)
    packed = pl.cdiv(tm, ch) * ch
    row = pl.BlockSpec((tm, d), lambda i, e, c: (i, 0))
    vec = pl.BlockSpec((1, d), lambda i, e, c: (0, 0))
    return pl.pallas_call(
        functools.partial(_moe_ln_body, alpha=alpha, n_experts=n_experts, n_rows=m, ch=ch),
        grid=(pl.cdiv(m, tm), n_experts, f // fc),
        in_specs=[row,
                  pl.BlockSpec((LANES, d), lambda i, e, c: (0, 0)),
                  pl.BlockSpec((None, None, d, fc), lambda i, e, c: (layer, e, 0, c)),
                  pl.BlockSpec((None, None, d, fc), lambda i, e, c: (layer, e, 0, c)),
                  pl.BlockSpec((None, None, fc, d), lambda i, e, c: (layer, e, c, 0)),
                  vec, vec],
        out_specs=row,
        out_shape=jax.ShapeDtypeStruct((m, d), F32),
        scratch_shapes=[pltpu.VMEM((tm, d), BF16),
                        pltpu.VMEM((LANES, tm), F32),
                        pltpu.VMEM((tm, LANES), F32),
                        pltpu.VMEM((tm, LANES), F32),
                        pltpu.VMEM((packed, d), BF16),
                        pltpu.VMEM((packed, d), F32),
                        pltpu.VMEM((tm, d), F32)],
        compiler_params=_params(3),
        name="moe_ln",
    )(x, wrt, w1, w3, w2, g, b)


def kernel(x_prompt, x_sample, cache_k, cache_v, page_table, w_qkv, w_o, ln1_g, ln1_b,
           ln2_g, ln2_b, ffn_w1, ffn_w3, ffn_w2, moe_router, moe_w1, moe_w3, moe_w2):
    n_batch, seq, d = x_prompt.shape
    n_seq = x_sample.shape[0]
    depth = w_qkv.shape[0]
    n_heads, hd = cache_k.shape[3], cache_k.shape[4]
    assert n_heads == N_HEADS and n_heads * hd == d and x_sample.shape[1] == 1
    assert seq % MOBA_BLOCK == 0 and n_heads % HEAD_GROUP == 0
    n_experts = moe_router.shape[2]
    alpha = (2.0 * depth) ** 0.25
    q_scale = hd ** -0.5
    slopes = jnp.exp2(-8.0 * jnp.arange(1, n_heads + 1, dtype=F32) / n_heads)
    cache_kt = jnp.transpose(cache_k, (0, 1, 3, 4, 2))
    cache_vt = jnp.transpose(cache_v, (0, 1, 3, 4, 2))
    mw1, mw3, mw2 = moe_w1.astype(BF16), moe_w3.astype(BF16), moe_w2.astype(BF16)

    xp = x_prompt.reshape(n_batch * seq, d)
    xs = x_sample.reshape(n_seq, d)
    new_ks, new_vs = [], []
    stacks = None
    for i in range(depth):
        kind = "moba" if i % 2 == 0 else "sb"
        wq, wk, wv = w_qkv[i, :, :d], w_qkv[i, :, d:2 * d], w_qkv[i, :, 2 * d:]
        q_scale_prompt = q_scale * LOG2E if kind == "moba" else q_scale
        kp, vp, kb, qt, vt, km = _qkv_prompt(
            xp, wk.astype(BF16), wq.T.astype(BF16), wk.T.astype(BF16), wv.T.astype(BF16),
            q_scale_prompt, n_batch, i, depth, stacks)
        stacks = (kp, vp)
        op = _prompt_attention(kind, qt, kb, vt, km, slopes, n_batch)

        qkv_s = _linear(xs, w_qkv[i].astype(BF16), d)
        qs = (qkv_s[:, :d] * q_scale).reshape(n_seq, n_heads, hd)
        ks_ = qkv_s[:, d:2 * d].reshape(n_seq, n_heads, hd)
        vs = qkv_s[:, 2 * d:].reshape(n_seq, n_heads, hd)
        os_ = _decode_attention(kind, i, qs, ks_, vs, cache_kt, cache_vt, page_table, slopes)

        wo = w_o[i].astype(BF16)
        g1, b1 = ln1_g[i].reshape(1, d), ln1_b[i].reshape(1, d)
        g2, b2 = ln2_g[i].reshape(1, d), ln2_b[i].reshape(1, d)
        xp = _wo_ln(op, xp, wo, g1, b1, alpha, 512)
        xs = _wo_ln(os_.reshape(n_seq, d), xs, wo, g1, b1, alpha, n_seq)
        j = i // 2
        if i % 2 == 0:
            w1, w3, w2 = ffn_w1[j].astype(BF16), ffn_w3[j].astype(BF16), ffn_w2[j].astype(BF16)
            xp = _ffn_ln(xp, w1, w3, w2, g2, b2, alpha, 256)
            xs = _ffn_ln(xs, w1, w3, w2, g2, b2, alpha, n_seq)
        else:
            wrt = jnp.zeros((LANES, d), F32).at[:n_experts].set(moe_router[j].T)
            xp = _moe_ln(xp, wrt, mw1, mw3, mw2, j, g2, b2, alpha, MOE_TILE, MOE_FF_CHUNK)
            xs = _moe_ln(xs, wrt, mw1, mw3, mw2, j, g2, b2, alpha, n_seq, MOE_FF_CHUNK)
        new_ks.append(ks_)
        new_vs.append(vs)

    def token_major(stack):
        return jnp.transpose(stack.reshape(depth, n_batch, n_heads, hd, seq), (0, 1, 4, 2, 3))

    def stack_sample(parts):
        return jnp.stack(parts).reshape(depth, n_seq, 1, n_heads, hd)

    return (xp.reshape(n_batch, seq, d), xs.reshape(n_seq, 1, d),
            token_major(stacks[0]), token_major(stacks[1]),
            stack_sample(new_ks), stack_sample(new_vs))
```
